```python
import math
import jax, jax.numpy as jnp
from jax import lax
import numpy as np

D_MODEL = 4096
BATCH = 2
SEQ = 8192
DEPTH = 2

PLE_DIM = 256
D_FF = 11008
ATTN_QK_DIM = 64
ATTN_V_DIM = 2 * ATTN_QK_DIM
ATTN_WIDTH = D_MODEL // 2
N_ATTN_HEADS = ATTN_WIDTH // ATTN_V_DIM
QK_WIDTH = N_ATTN_HEADS * 2 * ATTN_QK_DIM
CONV_WIDTH = D_MODEL - ATTN_WIDTH
CONV_GROUPS = 16
CONV_GROUP_DIM = CONV_WIDTH // CONV_GROUPS
CONV_K = 3
MIX_WIDTH = ATTN_WIDTH + CONV_WIDTH
SPLITS = (
    QK_WIDTH,
    2 * QK_WIDTH,
    2 * QK_WIDTH + ATTN_WIDTH,
    2 * QK_WIDTH + ATTN_WIDTH + CONV_WIDTH,
    2 * QK_WIDTH + ATTN_WIDTH + 2 * CONV_WIDTH,
)
IN_PROJ_WIDTH = 2 * QK_WIDTH + ATTN_WIDTH + 3 * CONV_WIDTH
Q_BLOCK = 128
EPS = 1e-6
HALF_STEP = 0.5

kernel_name = "hymba_diffattn_shortconv_macaron_ple"


def _rmsnorm(x, gain):
    x32 = x.astype(jnp.float32)
    y = x32 * lax.rsqrt(jnp.mean(x32 * x32, axis=-1, keepdims=True) + EPS)
    return y.astype(x.dtype) * gain


def _swiglu(x, w_gate, w_up, w_down):
    return (jax.nn.silu(x @ w_gate) * (x @ w_up)) @ w_down


def _alibi_slopes(n_heads):
    return jnp.exp2(-8.0 * jnp.arange(1, n_heads + 1, dtype=jnp.float32) / n_heads)


def _lambda_init(layer_idx):
    return 0.8 - 0.6 * math.exp(-0.3 * layer_idx)


def _diff_attention(q, k, v, lam, slopes):
    bsz, seq = q.shape[0], q.shape[1]
    n_blocks = seq // Q_BLOCK
    key_pos = jnp.arange(seq, dtype=jnp.int32)
    scale = ATTN_QK_DIM ** -0.5

    def one_block(blk):
        start = blk * Q_BLOCK
        qb = lax.dynamic_slice_in_dim(q, start, Q_BLOCK, axis=1)
        s = jnp.einsum('bqhmd,bkhmd->bhmqk', qb, k,
                       preferred_element_type=jnp.float32) * scale
        dist = (start + jnp.arange(Q_BLOCK, dtype=jnp.int32))[:, None] - key_pos[None, :]
        bias = -slopes[:, None, None] * dist.astype(jnp.float32)
        s = jnp.where(dist[None, None, None] >= 0, s + bias[None, :, None], -jnp.inf)
        prob = jax.nn.softmax(s, axis=-1)
        a = prob[:, :, 0] - lam * prob[:, :, 1]
        return jnp.einsum('bhqk,bkhd->bqhd', a.astype(v.dtype), v)

    out = lax.map(one_block, jnp.arange(n_blocks, dtype=jnp.int32))
    return jnp.moveaxis(out, 0, 1).reshape(bsz, seq, N_ATTN_HEADS, ATTN_V_DIM)


def _short_conv(u, w):
    seq = u.shape[1]
    up = jnp.pad(u, ((0, 0), (CONV_K - 1, 0), (0, 0)))
    acc = up[:, 0:seq] * w[0]
    for j in range(1, CONV_K):
        acc = acc + up[:, j:j + seq] * w[j]
    return acc


def setup_inputs(seed: int = 0) -> dict:
    key = jax.random.key(seed)
    ks = jax.random.split(key, 26)

    def dense(k, shape, fan_in):
        return jax.random.normal(k, shape, jnp.float32) * (fan_in ** -0.5)

    def gain(k, shape):
        return 1.0 + 0.1 * jax.random.normal(k, shape, jnp.float32)

    def small(k, shape, scale=0.1):
        return scale * jax.random.normal(k, shape, jnp.float32)

    L = DEPTH
    return {
        "x": jax.random.normal(ks[0], (BATCH, SEQ, D_MODEL), jnp.float32),
        "p": jax.random.normal(ks[1], (DEPTH, BATCH, SEQ, PLE_DIM), jnp.float32),
        "ffn1_norm": gain(ks[2], (L, D_MODEL)),
        "ffn1_w_gate": dense(ks[3], (L, D_MODEL, D_FF), D_MODEL),
        "ffn1_w_up": dense(ks[4], (L, D_MODEL, D_FF), D_MODEL),
        "ffn1_w_down": dense(ks[5], (L, D_FF, D_MODEL), D_FF),
        "mix_norm": gain(ks[6], (L, D_MODEL)),
        "w_in": dense(ks[7], (L, D_MODEL, IN_PROJ_WIDTH), D_MODEL),
        "q_norm": gain(ks[8], (L, ATTN_QK_DIM)),
        "k_norm": gain(ks[9], (L, ATTN_QK_DIM)),
        "lambda_q1": small(ks[10], (L, ATTN_QK_DIM)),
        "lambda_k1": small(ks[11], (L, ATTN_QK_DIM)),
        "lambda_q2": small(ks[12], (L, ATTN_QK_DIM)),
        "lambda_k2": small(ks[13], (L, ATTN_QK_DIM)),
        "attn_subln": gain(ks[14], (L, ATTN_V_DIM)),
        "conv_w": dense(ks[15], (L, CONV_K, CONV_WIDTH), CONV_K),
        "conv_norm": gain(ks[16], (L, CONV_GROUP_DIM)),
        "w_out": dense(ks[17], (L, MIX_WIDTH, D_MODEL), MIX_WIDTH),
        "ffn2_norm": gain(ks[18], (L, D_MODEL)),
        "ffn2_w_gate": dense(ks[19], (L, D_MODEL, D_FF), D_MODEL),
        "ffn2_w_up": dense(ks[20], (L, D_MODEL, D_FF), D_MODEL),
        "ffn2_w_down": dense(ks[21], (L, D_FF, D_MODEL), D_FF),
        "ple_w_proj": dense(ks[22], (L, PLE_DIM, D_MODEL), PLE_DIM),
        "ple_post_norm": gain(ks[23], (L, D_MODEL)),
        "ple_gate_norm": gain(ks[24], (L, D_MODEL)),
        "ple_w_gate": dense(ks[25], (L, D_MODEL, D_MODEL), D_MODEL),
    }


def reference(x, p, ffn1_norm, ffn1_w_gate, ffn1_w_up, ffn1_w_down,
              mix_norm, w_in, q_norm, k_norm, lambda_q1, lambda_k1, lambda_q2, lambda_k2,
              attn_subln, conv_w, conv_norm, w_out,
              ffn2_norm, ffn2_w_gate, ffn2_w_up, ffn2_w_down,
              ple_w_proj, ple_post_norm, ple_gate_norm, ple_w_gate):
    bsz, seq = x.shape[0], x.shape[1]
    slopes = _alibi_slopes(N_ATTN_HEADS)
    for i in range(DEPTH):
        x = x + HALF_STEP * _swiglu(_rmsnorm(x, ffn1_norm[i]),
                                    ffn1_w_gate[i], ffn1_w_up[i], ffn1_w_down[i])

        h = _rmsnorm(x, mix_norm[i])
        z = h @ w_in[i]
        q, k, v, g_b, g_c, u = jnp.split(z, SPLITS, axis=-1)

        q = _rmsnorm(q.reshape(bsz, seq, N_ATTN_HEADS, 2, ATTN_QK_DIM), q_norm[i])
        k = _rmsnorm(k.reshape(bsz, seq, N_ATTN_HEADS, 2, ATTN_QK_DIM), k_norm[i])
        v = v.reshape(bsz, seq, N_ATTN_HEADS, ATTN_V_DIM)
        lam_init = _lambda_init(i)
        lam = (jnp.exp(jnp.sum(lambda_q1[i].astype(jnp.float32) * lambda_k1[i].astype(jnp.float32)))
               - jnp.exp(jnp.sum(lambda_q2[i].astype(jnp.float32) * lambda_k2[i].astype(jnp.float32)))
               + lam_init)
        attn = _diff_attention(q, k, v, lam, slopes)
        attn = (_rmsnorm(attn, attn_subln[i]) * (1.0 - lam_init)).reshape(bsz, seq, ATTN_WIDTH)

        y = g_b * _short_conv(g_c * u, conv_w[i])
        y = _rmsnorm(y.reshape(bsz, seq, CONV_GROUPS, CONV_GROUP_DIM),
                     conv_norm[i]).reshape(bsz, seq, CONV_WIDTH)

        x = x + jnp.concatenate([attn, y], axis=-1) @ w_out[i]

        x = x + HALF_STEP * _swiglu(_rmsnorm(x, ffn2_norm[i]),
                                    ffn2_w_gate[i], ffn2_w_up[i], ffn2_w_down[i])

        e = _rmsnorm(p[i] @ ple_w_proj[i], ple_post_norm[i])
        gate = jax.nn.sigmoid(_rmsnorm(x, ple_gate_norm[i]) @ ple_w_gate[i])
        x = x + gate * e
    return x
```

```python
import functools
import math

import jax
import jax.numpy as jnp
from jax import lax
from jax.experimental import pallas as pl
from jax.experimental.pallas import tpu as pltpu

F32 = jnp.float32
BF16 = jnp.bfloat16

EPS = 1e-6
HALF_STEP = 0.5
QK_DIM = 64
V_DIM = 2 * QK_DIM
CONV_GROUP_DIM = 128
CONV_K = 3
PLE_DIM = 256
NEG_BIG = -1e30

MIB = 1024 * 1024


def _tile(dim, pref):
    t = min(dim, pref)
    while dim % t:
        t //= 2
    return t


def _params(semantics, vmem_mib):
    return pltpu.CompilerParams(dimension_semantics=semantics,
                                vmem_limit_bytes=vmem_mib * MIB)


def _norm_body(x_ref, g_ref, o_ref):
    x = x_ref[...]
    ms = jnp.mean(x * x, axis=-1, keepdims=True)
    o_ref[...] = ((x * lax.rsqrt(ms + EPS)) * g_ref[...]).astype(o_ref.dtype)


def _rmsnorm(x, gain):
    m, d = x.shape
    bm = _tile(m, 256)
    return pl.pallas_call(
        _norm_body,
        grid=(m // bm,),
        in_specs=[pl.BlockSpec((bm, d), lambda i: (i, 0)),
                  pl.BlockSpec((1, d), lambda i: (0, 0))],
        out_specs=pl.BlockSpec((bm, d), lambda i: (i, 0)),
        out_shape=jax.ShapeDtypeStruct((m, d), BF16),
        compiler_params=_params(("parallel",), 32),
        name="rmsnorm",
    )(x, gain.reshape(1, d))


def _ffn_body(h_ref, x_ref, wg_ref, wu_ref, wd_ref, o_ref, *, n_split):
    k = pl.program_id(1)

    @pl.when(k == 0)
    def _():
        o_ref[...] = jnp.zeros_like(o_ref)

    h = h_ref[...]
    g = jnp.dot(h, wg_ref[...], preferred_element_type=F32)
    u = jnp.dot(h, wu_ref[...], preferred_element_type=F32)
    a = ((g * jax.nn.sigmoid(g)) * u).astype(BF16)
    cols = o_ref.shape[1] // n_split
    for j in range(n_split):
        sl = slice(j * cols, (j + 1) * cols)
        o_ref[:, sl] += jnp.dot(a, wd_ref[:, sl], preferred_element_type=F32)

    @pl.when(k == pl.num_programs(1) - 1)
    def _():
        o_ref[...] = x_ref[...] + HALF_STEP * o_ref[...]


def _ffn(h, x, wg, wu, wd):
    m, d = x.shape
    f = wg.shape[1]
    bm = _tile(m, 512)
    ck = _tile(f, 256)
    once = pl.Buffered(1)
    return pl.pallas_call(
        functools.partial(_ffn_body, n_split=max(1, d // 1024)),
        grid=(m // bm, f // ck),
        in_specs=[pl.BlockSpec((bm, d), lambda i, k: (i, 0), pipeline_mode=once),
                  pl.BlockSpec((bm, d), lambda i, k: (i, 0), pipeline_mode=once),
                  pl.BlockSpec((d, ck), lambda i, k: (0, k)),
                  pl.BlockSpec((d, ck), lambda i, k: (0, k)),
                  pl.BlockSpec((ck, d), lambda i, k: (k, 0))],
        out_specs=pl.BlockSpec((bm, d), lambda i, k: (i, 0)),
        out_shape=jax.ShapeDtypeStruct((m, d), F32),
        compiler_params=_params(("parallel", "arbitrary"), 56),
        name="ffn",
    )(h, x, wg, wu, wd)


def _mm_body(h_ref, w_ref, o_ref):
    o_ref[...] = jnp.dot(h_ref[...], w_ref[...],
                         preferred_element_type=F32).astype(o_ref.dtype)


def _matmul(h, w, out_dtype):
    m, kd = h.shape
    n = w.shape[1]
    bm = _tile(m, 1024)
    bn = _tile(n, 512)
    return pl.pallas_call(
        _mm_body,
        grid=(m // bm, n // bn),
        in_specs=[pl.BlockSpec((bm, kd), lambda i, j: (i, 0)),
                  pl.BlockSpec((kd, bn), lambda i, j: (0, j))],
        out_specs=pl.BlockSpec((bm, bn), lambda i, j: (i, j)),
        out_shape=jax.ShapeDtypeStruct((m, n), out_dtype),
        compiler_params=_params(("parallel", "arbitrary"), 48),
        name="in_proj",
    )(h, w)


def _attn_body(scal_ref, slope_ref, q_ref, k_ref, v_ref, qg_ref, kg_ref, sg_ref, o_ref,
               kn_s, vb_s, m_s, l_s, acc_s, bias_s, *, blk):
    hd = pl.program_id(1)
    qi = pl.program_id(2)
    seq = k_ref.shape[0]
    lo = lax.broadcasted_iota(jnp.int32, (1, V_DIM), 1) < QK_DIM

    def group_norm(t, gain):
        t2 = t * t
        s_lo = jnp.sum(jnp.where(lo, t2, 0.0), axis=-1, keepdims=True)
        s_hi = jnp.sum(jnp.where(lo, 0.0, t2), axis=-1, keepdims=True)
        ms = jnp.where(lo, s_lo, s_hi) * (1.0 / QK_DIM)
        return (t * lax.rsqrt(ms + EPS)) * gain

    @pl.when(qi == 0)
    def _():
        def prep(c, carry):
            r = pl.multiple_of(c * blk, blk)
            kn_s[pl.ds(r, blk), :] = group_norm(k_ref[pl.ds(r, blk), :], kg_ref[...]).astype(BF16)
            vb_s[pl.ds(r, blk), :] = v_ref[pl.ds(r, blk), :].astype(BF16)
            return carry
        lax.fori_loop(0, seq // blk, prep, 0)

    qn = group_norm(q_ref[...], qg_ref[...]) * (QK_DIM ** -0.5)
    q_maps = (jnp.where(lo, qn, 0.0).astype(BF16), jnp.where(lo, 0.0, qn).astype(BF16))

    slope = slope_ref[hd]
    row = lax.broadcasted_iota(jnp.int32, (blk, blk), 0)
    col = lax.broadcasted_iota(jnp.int32, (blk, blk), 1)
    bias_s[...] = slope * (col - row).astype(F32)

    m_s[...] = jnp.full_like(m_s, NEG_BIG)
    l_s[...] = jnp.zeros_like(l_s)
    acc_s[...] = jnp.zeros_like(acc_s)

    def step(kblk, vblk, off, diagonal):
        for mi in range(2):
            s = lax.dot_general(q_maps[mi], kblk, (((1,), (1,)), ((), ())),
                                preferred_element_type=F32)
            s = s + bias_s[...] + off
            if diagonal:
                s = jnp.where(col <= row, s, -jnp.inf)
            m_prev = m_s[mi]
            m_new = jnp.maximum(m_prev, jnp.max(s, axis=-1, keepdims=True))
            alpha = jnp.exp(m_prev - m_new)
            p = jnp.exp(s - m_new)
            l_s[mi] = alpha * l_s[mi] + jnp.sum(p, axis=-1, keepdims=True)
            acc_s[mi] = alpha * acc_s[mi] + jnp.dot(p.astype(BF16), vblk,
                                                    preferred_element_type=F32)
            m_s[mi] = m_new

    def body(j, carry):
        r = pl.multiple_of(j * blk, blk)
        off = slope * ((j - qi) * blk).astype(F32)
        step(kn_s[pl.ds(r, blk), :], vb_s[pl.ds(r, blk), :], off, False)
        return carry
    lax.fori_loop(0, qi, body, 0)

    rd = pl.multiple_of(qi * blk, blk)
    step(kn_s[pl.ds(rd, blk), :], vb_s[pl.ds(rd, blk), :], 0.0, True)

    lam = scal_ref[0]
    out_scale = scal_ref[1]
    o = acc_s[0] / l_s[0] - lam * (acc_s[1] / l_s[1])
    ms = jnp.mean(o * o, axis=-1, keepdims=True)
    o_ref[...] = (((o * lax.rsqrt(ms + EPS)) * sg_ref[...]) * out_scale).astype(o_ref.dtype)


def _diff_attention(z, scal, slopes, q_gain, k_gain, subln_gain, n_heads):
    bsz, seq, _ = z.shape
    blk = _tile(seq, 512)
    smem = pl.BlockSpec(memory_space=pltpu.SMEM)
    gain_spec = pl.BlockSpec((1, V_DIM), lambda b, h, i: (0, 0))
    return pl.pallas_call(
        functools.partial(_attn_body, blk=blk),
        grid=(bsz, n_heads, seq // blk),
        in_specs=[smem, smem,
                  pl.BlockSpec((None, blk, V_DIM), lambda b, h, i: (b, i, h)),
                  pl.BlockSpec((None, seq, V_DIM), lambda b, h, i: (b, 0, n_heads + h)),
                  pl.BlockSpec((None, seq, V_DIM), lambda b, h, i: (b, 0, 2 * n_heads + h)),
                  gain_spec, gain_spec, gain_spec],
        out_specs=pl.BlockSpec((None, blk, V_DIM), lambda b, h, i: (b, i, h)),
        out_shape=jax.ShapeDtypeStruct((bsz, seq, n_heads * V_DIM), BF16),
        scratch_shapes=[pltpu.VMEM((seq, V_DIM), BF16),
                        pltpu.VMEM((seq, V_DIM), BF16),
                        pltpu.VMEM((2, blk, 1), F32),
                        pltpu.VMEM((2, blk, 1), F32),
                        pltpu.VMEM((2, blk, V_DIM), F32),
                        pltpu.VMEM((blk, blk), F32)],
        compiler_params=_params(("parallel", "parallel", "arbitrary"), 48),
        name="diff_attn",
    )(scal, slopes, z, z, z,
      jnp.tile(q_gain, 2).reshape(1, V_DIM), jnp.tile(k_gain, 2).reshape(1, V_DIM),
      subln_gain.reshape(1, V_DIM))


def _conv_body(b_ref, c_ref, u_ref, w_ref, g_ref, o_ref, tail_s):
    si = pl.program_id(2)
    rows, cols = o_ref.shape

    @pl.when(si == 0)
    def _():
        tail_s[...] = jnp.zeros_like(tail_s)

    cu = c_ref[...] * u_ref[...]
    row = lax.broadcasted_iota(jnp.int32, (rows, cols), 0)
    tail = tail_s[...]
    prev1 = jnp.where(row == 0, tail[7:8, :], pltpu.roll(cu, 1, axis=0))
    prev2 = jnp.where(row == 0, tail[6:7, :],
                      jnp.where(row == 1, tail[7:8, :], pltpu.roll(cu, 2, axis=0)))
    acc = prev2 * w_ref[0:1, :]
    acc = acc + prev1 * w_ref[1:2, :]
    acc = acc + cu * w_ref[2:3, :]
    y = b_ref[...] * acc
    tail_s[...] = cu[rows - 8:, :]
    for gi in range(cols // CONV_GROUP_DIM):
        sl = slice(gi * CONV_GROUP_DIM, (gi + 1) * CONV_GROUP_DIM)
        yg = y[:, sl]
        ms = jnp.mean(yg * yg, axis=-1, keepdims=True)
        o_ref[:, sl] = ((yg * lax.rsqrt(ms + EPS)) * g_ref[...]).astype(o_ref.dtype)


def _gated_conv(z, col0, width, conv_w, group_gain):
    bsz, seq, _ = z.shape
    cb = _tile(width, 512)
    bs = _tile(seq, 512)
    nb = width // cb
    base = col0 // cb
    zspec = lambda slab: pl.BlockSpec(
        (None, bs, cb), lambda b, c, s: (b, s, base + slab * nb + c))
    return pl.pallas_call(
        _conv_body,
        grid=(bsz, nb, seq // bs),
        in_specs=[zspec(0), zspec(1), zspec(2),
                  pl.BlockSpec((CONV_K, cb), lambda b, c, s: (0, c)),
                  pl.BlockSpec((1, CONV_GROUP_DIM), lambda b, c, s: (0, 0))],
        out_specs=pl.BlockSpec((None, bs, cb), lambda b, c, s: (b, s, c)),
        out_shape=jax.ShapeDtypeStruct((bsz, seq, width), BF16),
        scratch_shapes=[pltpu.VMEM((8, cb), F32)],
        compiler_params=_params(("parallel", "parallel", "arbitrary"), 32),
        name="gated_conv",
    )(z, z, z, conv_w, group_gain.reshape(1, CONV_GROUP_DIM))


def _out_proj_body(a_ref, y_ref, wa_ref, wc_ref, x_ref, o_ref):
    mix = (jnp.dot(a_ref[...], wa_ref[...], preferred_element_type=F32)
           + jnp.dot(y_ref[...], wc_ref[...], preferred_element_type=F32))
    o_ref[...] = x_ref[...] + mix


def _out_proj(attn, conv, w_attn, w_conv, x):
    m, d = x.shape
    bm = _tile(m, 1024)
    bn = _tile(d, 512)
    ka, kc = attn.shape[1], conv.shape[1]
    return pl.pallas_call(
        _out_proj_body,
        grid=(m // bm, d // bn),
        in_specs=[pl.BlockSpec((bm, ka), lambda i, j: (i, 0)),
                  pl.BlockSpec((bm, kc), lambda i, j: (i, 0)),
                  pl.BlockSpec((ka, bn), lambda i, j: (0, j)),
                  pl.BlockSpec((kc, bn), lambda i, j: (0, j)),
                  pl.BlockSpec((bm, bn), lambda i, j: (i, j))],
        out_specs=pl.BlockSpec((bm, bn), lambda i, j: (i, j)),
        out_shape=jax.ShapeDtypeStruct((m, d), F32),
        compiler_params=_params(("parallel", "arbitrary"), 48),
        name="out_proj",
    )(attn, conv, w_attn, w_conv, x)


def _ple_body(hg_ref, wg_ref, p_ref, wp_ref, pg_ref, x_ref, o_ref, e_s, inv_s):
    j = pl.program_id(1)
    n_tiles, _, bn = e_s.shape

    @pl.when(j == 0)
    def _():
        pb = p_ref[...].astype(BF16)
        ssq = jnp.zeros(inv_s.shape, F32)
        for t in range(n_tiles):
            e = jnp.dot(pb, wp_ref[:, t * bn:(t + 1) * bn], preferred_element_type=F32)
            e_s[t] = e
            ssq = ssq + jnp.sum(e * e, axis=-1, keepdims=True)
        inv_s[...] = lax.rsqrt(ssq / (n_tiles * bn) + EPS)

    gate = jax.nn.sigmoid(jnp.dot(hg_ref[...], wg_ref[...], preferred_element_type=F32))
    e = (e_s[j] * inv_s[...]) * pg_ref[...]
    o_ref[...] = x_ref[...] + gate * e


def _ple(hg, w_gate, p, w_proj, post_gain, x):
    m, d = x.shape
    pd = p.shape[1]
    bm = _tile(m, 512)
    bn = _tile(d, 512)
    return pl.pallas_call(
        _ple_body,
        grid=(m // bm, d // bn),
        in_specs=[pl.BlockSpec((bm, d), lambda i, j: (i, 0)),
                  pl.BlockSpec((d, bn), lambda i, j: (0, j)),
                  pl.BlockSpec((bm, pd), lambda i, j: (i, 0)),
                  pl.BlockSpec((pd, d), lambda i, j: (0, 0)),
                  pl.BlockSpec((1, bn), lambda i, j: (0, j)),
                  pl.BlockSpec((bm, bn), lambda i, j: (i, j))],
        out_specs=pl.BlockSpec((bm, bn), lambda i, j: (i, j)),
        out_shape=jax.ShapeDtypeStruct((m, d), F32),
        scratch_shapes=[pltpu.VMEM((d // bn, bm, bn), F32),
                        pltpu.VMEM((bm, 1), F32)],
        compiler_params=_params(("parallel", "arbitrary"), 48),
        name="ple",
    )(hg, w_gate, p, w_proj, post_gain.reshape(1, d), x)


def kernel(x, p, ffn1_norm, ffn1_w_gate, ffn1_w_up, ffn1_w_down, mix_norm, w_in, q_norm, k_norm, lambda_q1, lambda_k1, lambda_q2, lambda_k2, attn_subln, conv_w, conv_norm, w_out, ffn2_norm, ffn2_w_gate, ffn2_w_up, ffn2_w_down, ple_w_proj, ple_post_norm, ple_gate_norm, ple_w_gate):
    bsz, seq, d = x.shape
    depth = p.shape[0]
    m = bsz * seq
    attn_width = w_out.shape[1] - conv_w.shape[2]
    conv_width = conv_w.shape[2]
    n_heads = attn_width // V_DIM
    assert w_in.shape[2] == 3 * attn_width + 3 * conv_width

    slopes = jnp.exp2(-8.0 * jnp.arange(1, n_heads + 1, dtype=F32) / n_heads)
    xf = x.reshape(m, d)
    for i in range(depth):
        lam_init = 0.8 - 0.6 * math.exp(-0.3 * i)
        lam = (jnp.exp(jnp.sum(lambda_q1[i] * lambda_k1[i]))
               - jnp.exp(jnp.sum(lambda_q2[i] * lambda_k2[i])) + lam_init)
        scal = jnp.stack([lam, jnp.asarray(1.0 - lam_init, F32)]).astype(F32)

        h = _rmsnorm(xf, ffn1_norm[i])
        xf = _ffn(h, xf, ffn1_w_gate[i].astype(BF16), ffn1_w_up[i].astype(BF16),
                  ffn1_w_down[i].astype(BF16))

        h = _rmsnorm(xf, mix_norm[i])
        z = _matmul(h, w_in[i].astype(BF16), F32).reshape(bsz, seq, -1)
        attn = _diff_attention(z, scal, slopes, q_norm[i], k_norm[i], attn_subln[i], n_heads)
        conv = _gated_conv(z, 3 * attn_width, conv_width, conv_w[i], conv_norm[i])
        wo = w_out[i].astype(BF16)
        xf = _out_proj(attn.reshape(m, attn_width), conv.reshape(m, conv_width),
                       wo[:attn_width], wo[attn_width:], xf)

        h = _rmsnorm(xf, ffn2_norm[i])
        xf = _ffn(h, xf, ffn2_w_gate[i].astype(BF16), ffn2_w_up[i].astype(BF16),
                  ffn2_w_down[i].astype(BF16))

        hg = _rmsnorm(xf, ple_gate_norm[i])
        xf = _ple(hg, ple_w_gate[i].astype(BF16), p[i].reshape(m, -1),
                  ple_w_proj[i].astype(BF16), ple_post_norm[i], xf)
    return xf.reshape(bsz, seq, d)
```

```python
import functools
import math

import jax
import jax.numpy as jnp
from jax import lax
from jax.experimental import pallas as pl
from jax.experimental.pallas import tpu as pltpu

F32 = jnp.float32
BF16 = jnp.bfloat16

EPS = 1e-6
HALF_STEP = 0.5
QK_DIM = 64
V_DIM = 2 * QK_DIM
CONV_GROUP_DIM = 128
CONV_K = 3
PLE_DIM = 256
NEG_BIG = -1e30
LOG2E = 1.4426950408889634
ALIBI_COLS = 6
ONES_ROWS = 16
VT_ROWS = V_DIM + ONES_ROWS
ATTN_BLOCK = 512

MIB = 1024 * 1024


def _tile(dim, pref):
    t = min(dim, pref)
    while dim % t:
        t //= 2
    return t


def _params(semantics, vmem_mib):
    return pltpu.CompilerParams(dimension_semantics=semantics,
                                vmem_limit_bytes=vmem_mib * MIB)


def _norm_body(x_ref, g_ref, o_ref):
    x = x_ref[...]
    ms = jnp.mean(x * x, axis=-1, keepdims=True)
    o_ref[...] = ((x * lax.rsqrt(ms + EPS)) * g_ref[...]).astype(o_ref.dtype)


def _rmsnorm(x, gain):
    m, d = x.shape
    bm = _tile(m, 256)
    return pl.pallas_call(
        _norm_body,
        grid=(m // bm,),
        in_specs=[pl.BlockSpec((bm, d), lambda i: (i, 0)),
                  pl.BlockSpec((1, d), lambda i: (0, 0))],
        out_specs=pl.BlockSpec((bm, d), lambda i: (i, 0)),
        out_shape=jax.ShapeDtypeStruct((m, d), BF16),
        compiler_params=_params(("parallel",), 32),
        name="rmsnorm",
    )(x, gain.reshape(1, d))


def _ffn_body(h_ref, x_ref, wg_ref, wu_ref, wd_ref, o_ref, *, n_split):
    k = pl.program_id(1)

    @pl.when(k == 0)
    def _():
        o_ref[...] = jnp.zeros_like(o_ref)

    h = h_ref[...]
    g = jnp.dot(h, wg_ref[...], preferred_element_type=F32)
    u = jnp.dot(h, wu_ref[...], preferred_element_type=F32)
    a = ((g * jax.nn.sigmoid(g)) * u).astype(BF16)
    cols = o_ref.shape[1] // n_split
    for j in range(n_split):
        sl = slice(j * cols, (j + 1) * cols)
        o_ref[:, sl] += jnp.dot(a, wd_ref[:, sl], preferred_element_type=F32)

    @pl.when(k == pl.num_programs(1) - 1)
    def _():
        o_ref[...] = x_ref[...] + HALF_STEP * o_ref[...]


def _ffn(h, x, wg, wu, wd):
    m, d = x.shape
    f = wg.shape[1]
    bm = _tile(m, 512)
    ck = _tile(f, 256)
    once = pl.Buffered(1)
    return pl.pallas_call(
        functools.partial(_ffn_body, n_split=max(1, d // 1024)),
        grid=(m // bm, f // ck),
        in_specs=[pl.BlockSpec((bm, d), lambda i, k: (i, 0), pipeline_mode=once),
                  pl.BlockSpec((bm, d), lambda i, k: (i, 0), pipeline_mode=once),
                  pl.BlockSpec((d, ck), lambda i, k: (0, k)),
                  pl.BlockSpec((d, ck), lambda i, k: (0, k)),
                  pl.BlockSpec((ck, d), lambda i, k: (k, 0))],
        out_specs=pl.BlockSpec((bm, d), lambda i, k: (i, 0)),
        out_shape=jax.ShapeDtypeStruct((m, d), F32),
        compiler_params=_params(("parallel", "arbitrary"), 56),
        name="ffn",
    )(h, x, wg, wu, wd)


def _mm_body(h_ref, w_ref, o_ref):
    o_ref[...] = jnp.dot(h_ref[...], w_ref[...],
                         preferred_element_type=F32).astype(o_ref.dtype)


def _matmul(h, w, out_dtype):
    m, kd = h.shape
    n = w.shape[1]
    bm = _tile(m, 1024)
    bn = _tile(n, 512)
    return pl.pallas_call(
        _mm_body,
        grid=(m // bm, n // bn),
        in_specs=[pl.BlockSpec((bm, kd), lambda i, j: (i, 0)),
                  pl.BlockSpec((kd, bn), lambda i, j: (0, j))],
        out_specs=pl.BlockSpec((bm, bn), lambda i, j: (i, j)),
        out_shape=jax.ShapeDtypeStruct((m, n), out_dtype),
        compiler_params=_params(("parallel", "arbitrary"), 48),
        name="in_proj",
    )(h, w)


def _attn_prep_body(q_ref, k_ref, v_ref, qg_ref, kg_ref, qa_ref, qt_ref, kp_ref, vt_ref):
    rows = q_ref.shape[0]
    lane = lax.broadcasted_iota(jnp.int32, (1, V_DIM), 1)
    lo = lane < QK_DIM

    def group_norm(t, gain):
        t2 = t * t
        s_lo = jnp.sum(jnp.where(lo, t2, 0.0), axis=-1, keepdims=True)
        s_hi = jnp.sum(jnp.where(lo, 0.0, t2), axis=-1, keepdims=True)
        ms = jnp.where(lo, s_lo, s_hi) * (1.0 / QK_DIM)
        return (t * lax.rsqrt(ms + EPS)) * gain

    qn = group_norm(q_ref[...], qg_ref[...]) * (QK_DIM ** -0.5 * LOG2E)
    qa = qa_ref[...]
    qt_ref[0] = jnp.where(lo, qn, qa).T.astype(BF16)
    qt_ref[1] = jnp.where(lo, pltpu.roll(qn, QK_DIM, axis=1), qa).T.astype(BF16)

    kn = group_norm(k_ref[...], kg_ref[...])
    pos = lax.broadcasted_iota(jnp.int32, (rows, V_DIM), 0)
    pos_lo = (pos & 255).astype(F32)
    pos_hi = (pos - (pos & 255)).astype(F32)
    ka = jnp.where((lane >= QK_DIM) & (lane < QK_DIM + 3), pos_lo,
                   jnp.where((lane >= QK_DIM + 3) & (lane < QK_DIM + ALIBI_COLS), pos_hi, 0.0))
    kp_ref[0] = jnp.where(lo, kn, ka).astype(BF16)
    kp_ref[1] = jnp.where(lo, pltpu.roll(kn, QK_DIM, axis=1), ka).astype(BF16)

    vt_ref[0:V_DIM, :] = v_ref[...].T.astype(BF16)
    vt_ref[V_DIM:, :] = jnp.ones((ONES_ROWS, rows), BF16)


def _attn_prep(z, q_gain, k_gain, q_aug, n_heads, blk):
    bsz, seq, _ = z.shape
    nb = seq // blk
    gain_spec = pl.BlockSpec((1, V_DIM), lambda b, h, s: (0, 0))
    return pl.pallas_call(
        _attn_prep_body,
        grid=(bsz, n_heads, nb),
        in_specs=[pl.BlockSpec((None, blk, V_DIM), lambda b, h, s: (b, s, h)),
                  pl.BlockSpec((None, blk, V_DIM), lambda b, h, s: (b, s, n_heads + h)),
                  pl.BlockSpec((None, blk, V_DIM), lambda b, h, s: (b, s, 2 * n_heads + h)),
                  gain_spec, gain_spec,
                  pl.BlockSpec((None, 1, V_DIM), lambda b, h, s: (h, 0, 0))],
        out_specs=[pl.BlockSpec((None, None, 2, V_DIM, blk), lambda b, h, s: (b, h, 0, 0, s)),
                   pl.BlockSpec((None, None, 2, blk, V_DIM), lambda b, h, s: (b, h, 0, s, 0)),
                   pl.BlockSpec((None, None, None, VT_ROWS, blk), lambda b, h, s: (b, h, s, 0, 0))],
        out_shape=[jax.ShapeDtypeStruct((bsz, n_heads, 2, V_DIM, seq), BF16),
                   jax.ShapeDtypeStruct((bsz, n_heads, 2, seq, V_DIM), BF16),
                   jax.ShapeDtypeStruct((bsz, n_heads, nb, VT_ROWS, blk), BF16)],
        compiler_params=_params(("parallel", "parallel", "parallel"), 32),
        name="attn_prep",
    )(z, z, z, jnp.tile(q_gain, 2).reshape(1, V_DIM), jnp.tile(k_gain, 2).reshape(1, V_DIM), q_aug)


def _attn_body(scal_ref, sl_ref, qt_ref, k_ref, vt_ref, sg_ref, o_ref,
               m_s, acc_s, st_a, st_b, *, blk):
    hd = pl.program_id(1)
    qi = pl.program_id(2)
    sl = sl_ref[hd]
    m_s[...] = jnp.full_like(m_s, NEG_BIG)
    acc_s[...] = jnp.zeros_like(acc_s)
    st_bufs = (st_a, st_b)

    def scores(j, dst):
        r = pl.multiple_of(j * blk, blk)
        for mi in range(2):
            st_bufs[dst][mi] = jnp.dot(k_ref[mi, pl.ds(r, blk), :], qt_ref[mi],
                                       preferred_element_type=F32)

    def absorb(j, src, diagonal):
        off = sl * ((j - qi) * blk).astype(F32)
        for mi in range(2):
            st = st_bufs[src][mi]
            if diagonal:
                key = lax.broadcasted_iota(jnp.int32, st.shape, 0)
                qry = lax.broadcasted_iota(jnp.int32, st.shape, 1)
                st = jnp.where(key <= qry, st, -jnp.inf)
            m_old = m_s[mi]
            m_new = jnp.maximum(m_old, jnp.max(st, axis=0, keepdims=True) + off)
            alpha = jnp.exp2(m_old - m_new)
            pt = jnp.exp2(st - (m_new - off)).astype(BF16)
            acc_s[mi] = alpha * acc_s[mi] + jnp.dot(vt_ref[j], pt, preferred_element_type=F32)
            m_s[mi] = m_new

    scores(0, 0)

    def pair(t, carry):
        j = 2 * t
        scores(j + 1, 1)
        absorb(j, 0, False)
        scores(j + 2, 0)
        absorb(j + 1, 1, False)
        return carry
    lax.fori_loop(0, qi // 2, pair, 0)

    @pl.when(qi % 2 == 0)
    def _():
        absorb(qi, 0, True)

    @pl.when(qi % 2 == 1)
    def _():
        scores(qi, 1)
        absorb(qi - 1, 0, False)
        absorb(qi, 1, True)

    lam = scal_ref[0]
    a0 = acc_s[0]
    a1 = acc_s[1]
    o = a0[:V_DIM] / a0[V_DIM:V_DIM + 1] - lam * (a1[:V_DIM] / a1[V_DIM:V_DIM + 1])
    ms = jnp.mean(o * o, axis=0, keepdims=True)
    y = ((o * lax.rsqrt(ms + EPS)) * sg_ref[...]) * scal_ref[1]
    o_ref[...] = y.T.astype(o_ref.dtype)


def _alibi_columns(n_heads):
    slopes = jnp.exp2(-8.0 * jnp.arange(1, n_heads + 1, dtype=F32) / n_heads)
    sl = slopes * LOG2E
    a1 = sl.astype(BF16).astype(F32)
    a2 = (sl - a1).astype(BF16).astype(F32)
    a3 = (sl - a1 - a2).astype(BF16).astype(F32)
    pieces = jnp.stack([a1, a2, a3, a1, a2, a3], axis=1)
    q_aug = jnp.zeros((n_heads, 1, V_DIM), F32).at[:, 0, QK_DIM:QK_DIM + ALIBI_COLS].set(pieces)
    return a1 + a2 + a3, q_aug


def _diff_attention(z, scal, q_gain, k_gain, subln_gain, n_heads):
    bsz, seq, _ = z.shape
    blk = _tile(seq, ATTN_BLOCK)
    sl, q_aug = _alibi_columns(n_heads)
    qt, kp, vt = _attn_prep(z, q_gain, k_gain, q_aug, n_heads, blk)
    smem = pl.BlockSpec(memory_space=pltpu.SMEM)
    return pl.pallas_call(
        functools.partial(_attn_body, blk=blk),
        grid=(bsz, n_heads, seq // blk),
        in_specs=[smem, smem,
                  pl.BlockSpec((None, None, 2, V_DIM, blk), lambda b, h, i: (b, h, 0, 0, i)),
                  pl.BlockSpec((None, None, 2, seq, V_DIM), lambda b, h, i: (b, h, 0, 0, 0)),
                  pl.BlockSpec((None, None, seq // blk, VT_ROWS, blk), lambda b, h, i: (b, h, 0, 0, 0)),
                  pl.BlockSpec((V_DIM, 1), lambda b, h, i: (0, 0))],
        out_specs=pl.BlockSpec((None, blk, V_DIM), lambda b, h, i: (b, i, h)),
        out_shape=jax.ShapeDtypeStruct((bsz, seq, n_heads * V_DIM), BF16),
        scratch_shapes=[pltpu.VMEM((2, 1, blk), F32),
                        pltpu.VMEM((2, VT_ROWS, blk), F32),
                        pltpu.VMEM((2, blk, blk), F32),
                        pltpu.VMEM((2, blk, blk), F32)],
        compiler_params=_params(("parallel", "parallel", "arbitrary"), 48),
        name="diff_attn",
    )(scal, sl, qt, kp, vt, subln_gain.reshape(V_DIM, 1))


def _conv_body(b_ref, c_ref, u_ref, w_ref, g_ref, o_ref, tail_s):
    si = pl.program_id(2)
    rows, cols = o_ref.shape

    @pl.when(si == 0)
    def _():
        tail_s[...] = jnp.zeros_like(tail_s)

    cu = c_ref[...] * u_ref[...]
    row = lax.broadcasted_iota(jnp.int32, (rows, cols), 0)
    tail = tail_s[...]
    prev1 = jnp.where(row == 0, tail[7:8, :], pltpu.roll(cu, 1, axis=0))
    prev2 = jnp.where(row == 0, tail[6:7, :],
                      jnp.where(row == 1, tail[7:8, :], pltpu.roll(cu, 2, axis=0)))
    acc = prev2 * w_ref[0:1, :]
    acc = acc + prev1 * w_ref[1:2, :]
    acc = acc + cu * w_ref[2:3, :]
    y = b_ref[...] * acc
    tail_s[...] = cu[rows - 8:, :]
    for gi in range(cols // CONV_GROUP_DIM):
        sl = slice(gi * CONV_GROUP_DIM, (gi + 1) * CONV_GROUP_DIM)
        yg = y[:, sl]
        ms = jnp.mean(yg * yg, axis=-1, keepdims=True)
        o_ref[:, sl] = ((yg * lax.rsqrt(ms + EPS)) * g_ref[...]).astype(o_ref.dtype)


def _gated_conv(z, col0, width, conv_w, group_gain):
    bsz, seq, _ = z.shape
    cb = _tile(width, 512)
    bs = _tile(seq, 512)
    nb = width // cb
    base = col0 // cb
    zspec = lambda slab: pl.BlockSpec(
        (None, bs, cb), lambda b, c, s: (b, s, base + slab * nb + c))
    return pl.pallas_call(
        _conv_body,
        grid=(bsz, nb, seq // bs),
        in_specs=[zspec(0), zspec(1), zspec(2),
                  pl.BlockSpec((CONV_K, cb), lambda b, c, s: (0, c)),
                  pl.BlockSpec((1, CONV_GROUP_DIM), lambda b, c, s: (0, 0))],
        out_specs=pl.BlockSpec((None, bs, cb), lambda b, c, s: (b, s, c)),
        out_shape=jax.ShapeDtypeStruct((bsz, seq, width), BF16),
        scratch_shapes=[pltpu.VMEM((8, cb), F32)],
        compiler_params=_params(("parallel", "parallel", "arbitrary"), 32),
        name="gated_conv",
    )(z, z, z, conv_w, group_gain.reshape(1, CONV_GROUP_DIM))


def _out_proj_body(a_ref, y_ref, wa_ref, wc_ref, x_ref, o_ref):
    mix = (jnp.dot(a_ref[...], wa_ref[...], preferred_element_type=F32)
           + jnp.dot(y_ref[...], wc_ref[...], preferred_element_type=F32))
    o_ref[...] = x_ref[...] + mix


def _out_proj(attn, conv, w_attn, w_conv, x):
    m, d = x.shape
    bm = _tile(m, 1024)
    bn = _tile(d, 512)
    ka, kc = attn.shape[1], conv.shape[1]
    return pl.pallas_call(
        _out_proj_body,
        grid=(m // bm, d // bn),
        in_specs=[pl.BlockSpec((bm, ka), lambda i, j: (i, 0)),
                  pl.BlockSpec((bm, kc), lambda i, j: (i, 0)),
                  pl.BlockSpec((ka, bn), lambda i, j: (0, j)),
                  pl.BlockSpec((kc, bn), lambda i, j: (0, j)),
                  pl.BlockSpec((bm, bn), lambda i, j: (i, j))],
        out_specs=pl.BlockSpec((bm, bn), lambda i, j: (i, j)),
        out_shape=jax.ShapeDtypeStruct((m, d), F32),
        compiler_params=_params(("parallel", "arbitrary"), 48),
        name="out_proj",
    )(attn, conv, w_attn, w_conv, x)


def _ple_body(hg_ref, wg_ref, p_ref, wp_ref, pg_ref, x_ref, o_ref, e_s, inv_s):
    j = pl.program_id(1)
    n_tiles, _, bn = e_s.shape

    @pl.when(j == 0)
    def _():
        pb = p_ref[...].astype(BF16)
        ssq = jnp.zeros(inv_s.shape, F32)
        for t in range(n_tiles):
            e = jnp.dot(pb, wp_ref[:, t * bn:(t + 1) * bn], preferred_element_type=F32)
            e_s[t] = e
            ssq = ssq + jnp.sum(e * e, axis=-1, keepdims=True)
        inv_s[...] = lax.rsqrt(ssq / (n_tiles * bn) + EPS)

    gate = jax.nn.sigmoid(jnp.dot(hg_ref[...], wg_ref[...], preferred_element_type=F32))
    e = (e_s[j] * inv_s[...]) * pg_ref[...]
    o_ref[...] = x_ref[...] + gate * e


def _ple(hg, w_gate, p, w_proj, post_gain, x):
    m, d = x.shape
    pd = p.shape[1]
    bm = _tile(m, 512)
    bn = _tile(d, 512)
    return pl.pallas_call(
        _ple_body,
        grid=(m // bm, d // bn),
        in_specs=[pl.BlockSpec((bm, d), lambda i, j: (i, 0)),
                  pl.BlockSpec((d, bn), lambda i, j: (0, j)),
                  pl.BlockSpec((bm, pd), lambda i, j: (i, 0)),
                  pl.BlockSpec((pd, d), lambda i, j: (0, 0)),
                  pl.BlockSpec((1, bn), lambda i, j: (0, j)),
                  pl.BlockSpec((bm, bn), lambda i, j: (i, j))],
        out_specs=pl.BlockSpec((bm, bn), lambda i, j: (i, j)),
        out_shape=jax.ShapeDtypeStruct((m, d), F32),
        scratch_shapes=[pltpu.VMEM((d // bn, bm, bn), F32),
                        pltpu.VMEM((bm, 1), F32)],
        compiler_params=_params(("parallel", "arbitrary"), 48),
        name="ple",
    )(hg, w_gate, p, w_proj, post_gain.reshape(1, d), x)


def kernel(x, p, ffn1_norm, ffn1_w_gate, ffn1_w_up, ffn1_w_down, mix_norm, w_in, q_norm, k_norm, lambda_q1, lambda_k1, lambda_q2, lambda_k2, attn_subln, conv_w, conv_norm, w_out, ffn2_norm, ffn2_w_gate, ffn2_w_up, ffn2_w_down, ple_w_proj, ple_post_norm, ple_gate_norm, ple_w_gate):
    bsz, seq, d = x.shape
    depth = p.shape[0]
    m = bsz * seq
    attn_width = w_out.shape[1] - conv_w.shape[2]
    conv_width = conv_w.shape[2]
    n_heads = attn_width // V_DIM
    assert w_in.shape[2] == 3 * attn_width + 3 * conv_width

    xf = x.reshape(m, d)
    for i in range(depth):
        lam_init = 0.8 - 0.6 * math.exp(-0.3 * i)
        lam = (jnp.exp(jnp.sum(lambda_q1[i] * lambda_k1[i]))
               - jnp.exp(jnp.sum(lambda_q2[i] * lambda_k2[i])) + lam_init)
        scal = jnp.stack([lam, jnp.asarray(1.0 - lam_init, F32)]).astype(F32)

        h = _rmsnorm(xf, ffn1_norm[i])
        xf = _ffn(h, xf, ffn1_w_gate[i].astype(BF16), ffn1_w_up[i].astype(BF16),
                  ffn1_w_down[i].astype(BF16))

        h = _rmsnorm(xf, mix_norm[i])
        z = _matmul(h, w_in[i].astype(BF16), F32).reshape(bsz, seq, -1)
        attn = _diff_attention(z, scal, q_norm[i], k_norm[i], attn_subln[i], n_heads)
        conv = _gated_conv(z, 3 * attn_width, conv_width, conv_w[i], conv_norm[i])
        wo = w_out[i].astype(BF16)
        xf = _out_proj(attn.reshape(m, attn_width), conv.reshape(m, conv_width),
                       wo[:attn_width], wo[attn_width:], xf)

        h = _rmsnorm(xf, ffn2_norm[i])
        xf = _ffn(h, xf, ffn2_w_gate[i].astype(BF16), ffn2_w_up[i].astype(BF16),
                  ffn2_w_down[i].astype(BF16))

        hg = _rmsnorm(xf, ple_gate_norm[i])
        xf = _ple(hg, ple_w_gate[i].astype(BF16), p[i].reshape(m, -1),
                  ple_w_proj[i].astype(BF16), ple_post_norm[i], xf)
    return xf.reshape(bsz, seq, d)
```

```python
import functools
import math

import jax
import jax.numpy as jnp
from jax import lax
from jax.experimental import pallas as pl
from jax.experimental.pallas import tpu as pltpu

F32 = jnp.float32
BF16 = jnp.bfloat16

EPS = 1e-6
HALF_STEP = 0.5
QK_DIM = 64
V_DIM = 2 * QK_DIM
CONV_GROUP_DIM = 128
CONV_K = 3
PLE_DIM = 256
NEG_BIG = -1e30
LOG2E = 1.4426950408889634
ALIBI_COLS = 6
ONES_ROWS = 16
VT_ROWS = V_DIM + ONES_ROWS
ATTN_BLOCK = 512

MIB = 1024 * 1024


def _tile(dim, pref):
    t = min(dim, pref)
    while dim % t:
        t //= 2
    return t


def _params(semantics, vmem_mib):
    return pltpu.CompilerParams(dimension_semantics=semantics,
                                vmem_limit_bytes=vmem_mib * MIB)


def _norm_body(x_ref, g_ref, o_ref):
    x = x_ref[...]
    ms = jnp.mean(x * x, axis=-1, keepdims=True)
    o_ref[...] = ((x * lax.rsqrt(ms + EPS)) * g_ref[...]).astype(o_ref.dtype)


def _rmsnorm(x, gain):
    m, d = x.shape
    bm = _tile(m, 256)
    return pl.pallas_call(
        _norm_body,
        grid=(m // bm,),
        in_specs=[pl.BlockSpec((bm, d), lambda i: (i, 0)),
                  pl.BlockSpec((1, d), lambda i: (0, 0))],
        out_specs=pl.BlockSpec((bm, d), lambda i: (i, 0)),
        out_shape=jax.ShapeDtypeStruct((m, d), BF16),
        compiler_params=_params(("parallel",), 32),
        name="rmsnorm",
    )(x, gain.reshape(1, d))


def _cast_body(w_ref, o_ref):
    o_ref[...] = w_ref[...].astype(o_ref.dtype)


def _cast_layer(w_stack, layer):
    _, r, c = w_stack.shape
    br = _tile(r, 256)
    return pl.pallas_call(
        _cast_body,
        grid=(r // br,),
        in_specs=[pl.BlockSpec((None, br, c), lambda i: (layer, i, 0))],
        out_specs=pl.BlockSpec((br, c), lambda i: (i, 0)),
        out_shape=jax.ShapeDtypeStruct((r, c), BF16),
        compiler_params=_params(("parallel",), 40),
        name="cast_bf16",
    )(w_stack)


def _ffn_body(x_ref, gain_ref, wg_ref, wu_ref, wd_ref, o_ref, h_s, *, n_split):
    k = pl.program_id(1)

    @pl.when(k == 0)
    def _():
        x = x_ref[...]
        ms = jnp.mean(x * x, axis=-1, keepdims=True)
        h_s[...] = ((x * lax.rsqrt(ms + EPS)) * gain_ref[...]).astype(BF16)
        o_ref[...] = jnp.zeros_like(o_ref)

    h = h_s[...]
    g = jnp.dot(h, wg_ref[...], preferred_element_type=F32)
    u = jnp.dot(h, wu_ref[...], preferred_element_type=F32)
    a = ((g * jax.nn.sigmoid(g)) * u).astype(BF16)
    cols = o_ref.shape[1] // n_split
    for j in range(n_split):
        sl = slice(j * cols, (j + 1) * cols)
        o_ref[:, sl] += jnp.dot(a, wd_ref[:, sl], preferred_element_type=F32)

    @pl.when(k == pl.num_programs(1) - 1)
    def _():
        o_ref[...] = x_ref[...] + HALF_STEP * o_ref[...]


def _ffn(x, gain, wg, wu, wd):
    m, d = x.shape
    f = wg.shape[1]
    bm = _tile(m, 512)
    ck = _tile(f, 256)
    return pl.pallas_call(
        functools.partial(_ffn_body, n_split=max(1, d // 1024)),
        grid=(m // bm, f // ck),
        in_specs=[pl.BlockSpec((bm, d), lambda i, k: (i, 0), pipeline_mode=pl.Buffered(1)),
                  pl.BlockSpec((1, d), lambda i, k: (0, 0)),
                  pl.BlockSpec((d, ck), lambda i, k: (0, k)),
                  pl.BlockSpec((d, ck), lambda i, k: (0, k)),
                  pl.BlockSpec((ck, d), lambda i, k: (k, 0))],
        out_specs=pl.BlockSpec((bm, d), lambda i, k: (i, 0)),
        out_shape=jax.ShapeDtypeStruct((m, d), F32),
        scratch_shapes=[pltpu.VMEM((bm, d), BF16)],
        compiler_params=_params(("parallel", "arbitrary"), 56),
        name="ffn",
    )(x, gain.reshape(1, d), wg, wu, wd)


def _in_proj_body(h_ref, w_ref, o_ref):
    o_ref[...] = jnp.dot(h_ref[...], w_ref[...].astype(BF16), preferred_element_type=F32)


def _in_proj(h, w_stack, layer):
    m, kd = h.shape
    n = w_stack.shape[2]
    bm = _tile(m, 1024)
    bn = _tile(n, 512)
    return pl.pallas_call(
        _in_proj_body,
        grid=(m // bm, n // bn),
        in_specs=[pl.BlockSpec((bm, kd), lambda i, j: (i, 0)),
                  pl.BlockSpec((None, kd, bn), lambda i, j: (layer, 0, j))],
        out_specs=pl.BlockSpec((bm, bn), lambda i, j: (i, j)),
        out_shape=jax.ShapeDtypeStruct((m, n), F32),
        compiler_params=_params(("parallel", "arbitrary"), 52),
        name="in_proj",
    )(h, w_stack)


def _attn_prep_body(q_ref, k_ref, v_ref, qg_ref, kg_ref, qa_ref, qt_ref, kp_ref, vt_ref):
    rows = q_ref.shape[0]
    lane = lax.broadcasted_iota(jnp.int32, (1, V_DIM), 1)
    lo = lane < QK_DIM

    def group_norm(t, gain):
        t2 = t * t
        s_lo = jnp.sum(jnp.where(lo, t2, 0.0), axis=-1, keepdims=True)
        s_hi = jnp.sum(jnp.where(lo, 0.0, t2), axis=-1, keepdims=True)
        ms = jnp.where(lo, s_lo, s_hi) * (1.0 / QK_DIM)
        return (t * lax.rsqrt(ms + EPS)) * gain

    qn = group_norm(q_ref[...], qg_ref[...]) * (QK_DIM ** -0.5 * LOG2E)
    qa = qa_ref[...]
    qt_ref[0] = jnp.where(lo, qn, qa).T.astype(BF16)
    qt_ref[1] = jnp.where(lo, pltpu.roll(qn, QK_DIM, axis=1), qa).T.astype(BF16)

    kn = group_norm(k_ref[...], kg_ref[...])
    pos = lax.broadcasted_iota(jnp.int32, (rows, V_DIM), 0)
    pos_lo = (pos & 255).astype(F32)
    pos_hi = (pos - (pos & 255)).astype(F32)
    ka = jnp.where((lane >= QK_DIM) & (lane < QK_DIM + 3), pos_lo,
                   jnp.where((lane >= QK_DIM + 3) & (lane < QK_DIM + ALIBI_COLS), pos_hi, 0.0))
    kp_ref[0] = jnp.where(lo, kn, ka).astype(BF16)
    kp_ref[1] = jnp.where(lo, pltpu.roll(kn, QK_DIM, axis=1), ka).astype(BF16)

    vt_ref[0:V_DIM, :] = v_ref[...].T.astype(BF16)
    vt_ref[V_DIM:, :] = jnp.ones((ONES_ROWS, rows), BF16)


def _attn_prep(z, q_gain, k_gain, q_aug, n_heads, blk):
    bsz, seq, _ = z.shape
    nb = seq // blk
    gain_spec = pl.BlockSpec((1, V_DIM), lambda b, h, s: (0, 0))
    return pl.pallas_call(
        _attn_prep_body,
        grid=(bsz, n_heads, nb),
        in_specs=[pl.BlockSpec((None, blk, V_DIM), lambda b, h, s: (b, s, h)),
                  pl.BlockSpec((None, blk, V_DIM), lambda b, h, s: (b, s, n_heads + h)),
                  pl.BlockSpec((None, blk, V_DIM), lambda b, h, s: (b, s, 2 * n_heads + h)),
                  gain_spec, gain_spec,
                  pl.BlockSpec((None, 1, V_DIM), lambda b, h, s: (h, 0, 0))],
        out_specs=[pl.BlockSpec((None, None, 2, V_DIM, blk), lambda b, h, s: (b, h, 0, 0, s)),
                   pl.BlockSpec((None, None, 2, blk, V_DIM), lambda b, h, s: (b, h, 0, s, 0)),
                   pl.BlockSpec((None, None, None, VT_ROWS, blk), lambda b, h, s: (b, h, s, 0, 0))],
        out_shape=[jax.ShapeDtypeStruct((bsz, n_heads, 2, V_DIM, seq), BF16),
                   jax.ShapeDtypeStruct((bsz, n_heads, 2, seq, V_DIM), BF16),
                   jax.ShapeDtypeStruct((bsz, n_heads, nb, VT_ROWS, blk), BF16)],
        compiler_params=_params(("parallel", "parallel", "parallel"), 32),
        name="attn_prep",
    )(z, z, z, jnp.tile(q_gain, 2).reshape(1, V_DIM), jnp.tile(k_gain, 2).reshape(1, V_DIM), q_aug)


def _attn_body(scal_ref, sl_ref, qt_ref, k_ref, vt_ref, sg_ref, o_ref,
               m_s, acc_s, st_a, st_b, *, blk):
    hd = pl.program_id(1)
    qi = pl.program_id(2)
    sl = sl_ref[hd]
    m_s[...] = jnp.full_like(m_s, NEG_BIG)
    acc_s[...] = jnp.zeros_like(acc_s)
    st_bufs = (st_a, st_b)

    def scores(j, dst):
        r = pl.multiple_of(j * blk, blk)
        for mi in range(2):
            st_bufs[dst][mi] = jnp.dot(k_ref[mi, pl.ds(r, blk), :], qt_ref[mi],
                                       preferred_element_type=F32)

    def absorb(j, src, diagonal):
        off = sl * ((j - qi) * blk).astype(F32)
        for mi in range(2):
            st = st_bufs[src][mi]
            if diagonal:
                key = lax.broadcasted_iota(jnp.int32, st.shape, 0)
                qry = lax.broadcasted_iota(jnp.int32, st.shape, 1)
                st = jnp.where(key <= qry, st, -jnp.inf)
            m_old = m_s[mi]
            m_new = jnp.maximum(m_old, jnp.max(st, axis=0, keepdims=True) + off)
            alpha = jnp.exp2(m_old - m_new)
            pt = jnp.exp2(st - (m_new - off)).astype(BF16)
            acc_s[mi] = alpha * acc_s[mi] + jnp.dot(vt_ref[j], pt, preferred_element_type=F32)
            m_s[mi] = m_new

    scores(0, 0)

    def pair(t, carry):
        j = 2 * t
        scores(j + 1, 1)
        absorb(j, 0, False)
        scores(j + 2, 0)
        absorb(j + 1, 1, False)
        return carry
    lax.fori_loop(0, qi // 2, pair, 0)

    @pl.when(qi % 2 == 0)
    def _():
        absorb(qi, 0, True)

    @pl.when(qi % 2 == 1)
    def _():
        scores(qi, 1)
        absorb(qi - 1, 0, False)
        absorb(qi, 1, True)

    lam = scal_ref[0]
    a0 = acc_s[0]
    a1 = acc_s[1]
    o = a0[:V_DIM] / a0[V_DIM:V_DIM + 1] - lam * (a1[:V_DIM] / a1[V_DIM:V_DIM + 1])
    ms = jnp.mean(o * o, axis=0, keepdims=True)
    y = ((o * lax.rsqrt(ms + EPS)) * sg_ref[...]) * scal_ref[1]
    o_ref[...] = y.T.astype(o_ref.dtype)


def _bf16_part(x):
    bits = lax.bitcast_convert_type(x, jnp.uint32) & jnp.uint32(0xFFFF0000)
    return lax.bitcast_convert_type(bits, F32)


def _alibi_columns(n_heads):
    slopes = jnp.exp2(-8.0 * jnp.arange(1, n_heads + 1, dtype=F32) / n_heads)
    sl = slopes * LOG2E
    a1 = _bf16_part(sl)
    a2 = _bf16_part(sl - a1)
    a3 = _bf16_part(sl - a1 - a2)
    pieces = jnp.stack([a1, a2, a3, a1, a2, a3], axis=1)
    q_aug = jnp.zeros((n_heads, 1, V_DIM), F32).at[:, 0, QK_DIM:QK_DIM + ALIBI_COLS].set(pieces)
    return a1 + a2 + a3, q_aug


def _diff_attention(z, scal, q_gain, k_gain, subln_gain, n_heads):
    bsz, seq, _ = z.shape
    blk = _tile(seq, ATTN_BLOCK)
    sl, q_aug = _alibi_columns(n_heads)
    qt, kp, vt = _attn_prep(z, q_gain, k_gain, q_aug, n_heads, blk)
    smem = pl.BlockSpec(memory_space=pltpu.SMEM)
    return pl.pallas_call(
        functools.partial(_attn_body, blk=blk),
        grid=(bsz, n_heads, seq // blk),
        in_specs=[smem, smem,
                  pl.BlockSpec((None, None, 2, V_DIM, blk), lambda b, h, i: (b, h, 0, 0, i)),
                  pl.BlockSpec((None, None, 2, seq, V_DIM), lambda b, h, i: (b, h, 0, 0, 0)),
                  pl.BlockSpec((None, None, seq // blk, VT_ROWS, blk), lambda b, h, i: (b, h, 0, 0, 0)),
                  pl.BlockSpec((V_DIM, 1), lambda b, h, i: (0, 0))],
        out_specs=pl.BlockSpec((None, blk, V_DIM), lambda b, h, i: (b, i, h)),
        out_shape=jax.ShapeDtypeStruct((bsz, seq, n_heads * V_DIM), BF16),
        scratch_shapes=[pltpu.VMEM((2, 1, blk), F32),
                        pltpu.VMEM((2, VT_ROWS, blk), F32),
                        pltpu.VMEM((2, blk, blk), F32),
                        pltpu.VMEM((2, blk, blk), F32)],
        compiler_params=_params(("parallel", "parallel", "arbitrary"), 48),
        name="diff_attn",
    )(scal, sl, qt, kp, vt, subln_gain.reshape(V_DIM, 1))


def _conv_body(b_ref, c_ref, u_ref, w_ref, g_ref, o_ref, tail_s):
    si = pl.program_id(2)
    rows, cols = o_ref.shape

    @pl.when(si == 0)
    def _():
        tail_s[...] = jnp.zeros_like(tail_s)

    cu = c_ref[...] * u_ref[...]
    row = lax.broadcasted_iota(jnp.int32, (rows, cols), 0)
    tail = tail_s[...]
    prev1 = jnp.where(row == 0, tail[7:8, :], pltpu.roll(cu, 1, axis=0))
    prev2 = jnp.where(row == 0, tail[6:7, :],
                      jnp.where(row == 1, tail[7:8, :], pltpu.roll(cu, 2, axis=0)))
    acc = prev2 * w_ref[0:1, :]
    acc = acc + prev1 * w_ref[1:2, :]
    acc = acc + cu * w_ref[2:3, :]
    y = b_ref[...] * acc
    tail_s[...] = cu[rows - 8:, :]
    for gi in range(cols // CONV_GROUP_DIM):
        sl = slice(gi * CONV_GROUP_DIM, (gi + 1) * CONV_GROUP_DIM)
        yg = y[:, sl]
        ms = jnp.mean(yg * yg, axis=-1, keepdims=True)
        o_ref[:, sl] = ((yg * lax.rsqrt(ms + EPS)) * g_ref[...]).astype(o_ref.dtype)


def _gated_conv(z, col0, width, conv_w, group_gain):
    bsz, seq, _ = z.shape
    cb = _tile(width, 512)
    bs = _tile(seq, 512)
    nb = width // cb
    base = col0 // cb
    zspec = lambda slab: pl.BlockSpec(
        (None, bs, cb), lambda b, c, s: (b, s, base + slab * nb + c))
    return pl.pallas_call(
        _conv_body,
        grid=(bsz, nb, seq // bs),
        in_specs=[zspec(0), zspec(1), zspec(2),
                  pl.BlockSpec((CONV_K, cb), lambda b, c, s: (0, c)),
                  pl.BlockSpec((1, CONV_GROUP_DIM), lambda b, c, s: (0, 0))],
        out_specs=pl.BlockSpec((None, bs, cb), lambda b, c, s: (b, s, c)),
        out_shape=jax.ShapeDtypeStruct((bsz, seq, width), BF16),
        scratch_shapes=[pltpu.VMEM((8, cb), F32)],
        compiler_params=_params(("parallel", "parallel", "arbitrary"), 32),
        name="gated_conv",
    )(z, z, z, conv_w, group_gain.reshape(1, CONV_GROUP_DIM))


def _out_proj_body(a_ref, y_ref, wa_ref, wc_ref, x_ref, o_ref):
    mix = (jnp.dot(a_ref[...], wa_ref[...].astype(BF16), preferred_element_type=F32)
           + jnp.dot(y_ref[...], wc_ref[...].astype(BF16), preferred_element_type=F32))
    o_ref[...] = x_ref[...] + mix


def _out_proj(attn, conv, w_stack, layer, x):
    m, d = x.shape
    bm = _tile(m, 1024)
    bn = _tile(d, 512)
    ka, kc = attn.shape[1], conv.shape[1]
    assert ka == kc, "row blocks of the output projection are indexed in units of one head group"
    return pl.pallas_call(
        _out_proj_body,
        grid=(m // bm, d // bn),
        in_specs=[pl.BlockSpec((bm, ka), lambda i, j: (i, 0)),
                  pl.BlockSpec((bm, kc), lambda i, j: (i, 0)),
                  pl.BlockSpec((None, ka, bn), lambda i, j: (layer, 0, j)),
                  pl.BlockSpec((None, kc, bn), lambda i, j: (layer, 1, j)),
                  pl.BlockSpec((bm, bn), lambda i, j: (i, j))],
        out_specs=pl.BlockSpec((bm, bn), lambda i, j: (i, j)),
        out_shape=jax.ShapeDtypeStruct((m, d), F32),
        compiler_params=_params(("parallel", "arbitrary"), 52),
        name="out_proj",
    )(attn, conv, w_stack, w_stack, x)


def _ple_body(hg_ref, wg_ref, p_ref, wp_ref, pg_ref, x_ref, o_ref, e_s, inv_s):
    j = pl.program_id(1)
    n_tiles, _, bn = e_s.shape

    @pl.when(j == 0)
    def _():
        pb = p_ref[...].astype(BF16)
        ssq = jnp.zeros(inv_s.shape, F32)
        for t in range(n_tiles):
            e = jnp.dot(pb, wp_ref[:, t * bn:(t + 1) * bn].astype(BF16),
                        preferred_element_type=F32)
            e_s[t] = e
            ssq = ssq + jnp.sum(e * e, axis=-1, keepdims=True)
        inv_s[...] = lax.rsqrt(ssq / (n_tiles * bn) + EPS)

    gate = jax.nn.sigmoid(jnp.dot(hg_ref[...], wg_ref[...].astype(BF16),
                                  preferred_element_type=F32))
    e = (e_s[j] * inv_s[...]) * pg_ref[...]
    o_ref[...] = x_ref[...] + gate * e


def _ple(hg, w_gate_stack, p, w_proj_stack, layer, post_gain, x):
    m, d = x.shape
    pd = p.shape[1]
    bm = _tile(m, 512)
    bn = _tile(d, 512)
    return pl.pallas_call(
        _ple_body,
        grid=(m // bm, d // bn),
        in_specs=[pl.BlockSpec((bm, d), lambda i, j: (i, 0)),
                  pl.BlockSpec((None, d, bn), lambda i, j: (layer, 0, j)),
                  pl.BlockSpec((bm, pd), lambda i, j: (i, 0)),
                  pl.BlockSpec((None, pd, d), lambda i, j: (layer, 0, 0)),
                  pl.BlockSpec((1, bn), lambda i, j: (0, j)),
                  pl.BlockSpec((bm, bn), lambda i, j: (i, j))],
        out_specs=pl.BlockSpec((bm, bn), lambda i, j: (i, j)),
        out_shape=jax.ShapeDtypeStruct((m, d), F32),
        scratch_shapes=[pltpu.VMEM((d // bn, bm, bn), F32),
                        pltpu.VMEM((bm, 1), F32)],
        compiler_params=_params(("parallel", "arbitrary"), 56),
        name="ple",
    )(hg, w_gate_stack, p, w_proj_stack, post_gain.reshape(1, d), x)


def kernel(x, p, ffn1_norm, ffn1_w_gate, ffn1_w_up, ffn1_w_down, mix_norm, w_in, q_norm, k_norm, lambda_q1, lambda_k1, lambda_q2, lambda_k2, attn_subln, conv_w, conv_norm, w_out, ffn2_norm, ffn2_w_gate, ffn2_w_up, ffn2_w_down, ple_w_proj, ple_post_norm, ple_gate_norm, ple_w_gate):
    bsz, seq, d = x.shape
    depth = p.shape[0]
    m = bsz * seq
    attn_width = w_out.shape[1] - conv_w.shape[2]
    conv_width = conv_w.shape[2]
    n_heads = attn_width // V_DIM
    assert w_in.shape[2] == 3 * attn_width + 3 * conv_width

    xf = x.reshape(m, d)
    for i in range(depth):
        lam_init = 0.8 - 0.6 * math.exp(-0.3 * i)
        lam = (jnp.exp(jnp.sum(lambda_q1[i] * lambda_k1[i]))
               - jnp.exp(jnp.sum(lambda_q2[i] * lambda_k2[i])) + lam_init)
        scal = jnp.stack([lam, jnp.asarray(1.0 - lam_init, F32)]).astype(F32)

        xf = _ffn(xf, ffn1_norm[i], _cast_layer(ffn1_w_gate, i), _cast_layer(ffn1_w_up, i),
                  _cast_layer(ffn1_w_down, i))

        h = _rmsnorm(xf, mix_norm[i])
        z = _in_proj(h, w_in, i).reshape(bsz, seq, -1)
        attn = _diff_attention(z, scal, q_norm[i], k_norm[i], attn_subln[i], n_heads)
        conv = _gated_conv(z, 3 * attn_width, conv_width, conv_w[i], conv_norm[i])
        xf = _out_proj(attn.reshape(m, attn_width), conv.reshape(m, conv_width), w_out, i, xf)

        xf = _ffn(xf, ffn2_norm[i], _cast_layer(ffn2_w_gate, i), _cast_layer(ffn2_w_up, i),
                  _cast_layer(ffn2_w_down, i))

        hg = _rmsnorm(xf, ple_gate_norm[i])
        xf = _ple(hg, ple_w_gate, p[i].reshape(m, -1), ple_w_proj, i, ple_post_norm[i], xf)
    return xf.reshape(bsz, seq, d)
```

```python
import functools
import math

import jax
import jax.numpy as jnp
from jax import lax
from jax.experimental import pallas as pl
from jax.experimental.pallas import tpu as pltpu

F32 = jnp.float32
BF16 = jnp.bfloat16

EPS = 1e-6
HALF_STEP = 0.5
QK_DIM = 64
V_DIM = 2 * QK_DIM
CONV_GROUP_DIM = 128
CONV_K = 3
PLE_DIM = 256
NEG_BIG = -1e30
LOG2E = 1.4426950408889634
ALIBI_COLS = 6
ONES_ROWS = 16
VT_ROWS = V_DIM + ONES_ROWS
ATTN_BLOCK = 512

MIB = 1024 * 1024


def _tile(dim, pref):
    t = min(dim, pref)
    while dim % t:
        t //= 2
    return t


def _params(semantics, vmem_mib):
    return pltpu.CompilerParams(dimension_semantics=semantics,
                                vmem_limit_bytes=vmem_mib * MIB)


def _norm_body(x_ref, g_ref, o_ref):
    x = x_ref[...]
    ms = jnp.mean(x * x, axis=-1, keepdims=True)
    o_ref[...] = ((x * lax.rsqrt(ms + EPS)) * g_ref[...]).astype(o_ref.dtype)


def _rmsnorm(x, gain):
    m, d = x.shape
    bm = _tile(m, 256)
    return pl.pallas_call(
        _norm_body,
        grid=(m // bm,),
        in_specs=[pl.BlockSpec((bm, d), lambda i: (i, 0)),
                  pl.BlockSpec((1, d), lambda i: (0, 0))],
        out_specs=pl.BlockSpec((bm, d), lambda i: (i, 0)),
        out_shape=jax.ShapeDtypeStruct((m, d), BF16),
        compiler_params=_params(("parallel",), 32),
        name="rmsnorm",
    )(x, gain.reshape(1, d))


def _cast_body(w_ref, o_ref):
    o_ref[...] = w_ref[...].astype(o_ref.dtype)


def _cast_layer(w_stack, layer):
    _, r, c = w_stack.shape
    br = _tile(r, 256)
    return pl.pallas_call(
        _cast_body,
        grid=(r // br,),
        in_specs=[pl.BlockSpec((None, br, c), lambda i: (layer, i, 0))],
        out_specs=pl.BlockSpec((br, c), lambda i: (i, 0)),
        out_shape=jax.ShapeDtypeStruct((r, c), BF16),
        compiler_params=_params(("parallel",), 40),
        name="cast_bf16",
    )(w_stack)


def _gate_up_body(h_ref, wg_ref, wu_ref, a_ref):
    h = h_ref[...]
    g = jnp.dot(h, wg_ref[...].astype(BF16), preferred_element_type=F32)
    u = jnp.dot(h, wu_ref[...].astype(BF16), preferred_element_type=F32)
    a_ref[...] = ((g * jax.nn.sigmoid(g)) * u).astype(a_ref.dtype)


def _gate_up(h, wg_stack, wu_stack, layer):
    m, d = h.shape
    f = wg_stack.shape[2]
    bm = _tile(m, 1024)
    bf = _tile(f, 256)
    wspec = pl.BlockSpec((None, d, bf), lambda i, k: (layer, 0, k))
    return pl.pallas_call(
        _gate_up_body,
        grid=(m // bm, f // bf),
        in_specs=[pl.BlockSpec((bm, d), lambda i, k: (i, 0)), wspec, wspec],
        out_specs=pl.BlockSpec((bm, bf), lambda i, k: (i, k)),
        out_shape=jax.ShapeDtypeStruct((m, f), BF16),
        compiler_params=_params(("parallel", "arbitrary"), 48),
        name="ffn_gate_up",
    )(h, wg_stack, wu_stack)


def _down_body(a_ref, wd_ref, x_ref, o_ref):
    o_ref[...] = x_ref[...] + HALF_STEP * jnp.dot(a_ref[...], wd_ref[...],
                                                  preferred_element_type=F32)


def _down(a, wd, x):
    m, d = x.shape
    f = a.shape[1]
    bm = _tile(m, 512)
    bn = _tile(d, 512)
    return pl.pallas_call(
        _down_body,
        grid=(m // bm, d // bn),
        in_specs=[pl.BlockSpec((bm, f), lambda i, j: (i, 0)),
                  pl.BlockSpec((f, bn), lambda i, j: (0, j)),
                  pl.BlockSpec((bm, bn), lambda i, j: (i, j))],
        out_specs=pl.BlockSpec((bm, bn), lambda i, j: (i, j)),
        out_shape=jax.ShapeDtypeStruct((m, d), F32),
        compiler_params=_params(("parallel", "arbitrary"), 56),
        name="ffn_down",
    )(a, wd, x)


def _ffn(x, gain, wg_stack, wu_stack, wd_stack, layer):
    a = _gate_up(_rmsnorm(x, gain), wg_stack, wu_stack, layer)
    return _down(a, _cast_layer(wd_stack, layer), x)


def _in_proj_body(h_ref, w_ref, o_ref):
    o_ref[...] = jnp.dot(h_ref[...], w_ref[...].astype(BF16), preferred_element_type=F32)


def _in_proj(h, w_stack, layer):
    m, kd = h.shape
    n = w_stack.shape[2]
    bm = _tile(m, 1024)
    bn = _tile(n, 512)
    return pl.pallas_call(
        _in_proj_body,
        grid=(m // bm, n // bn),
        in_specs=[pl.BlockSpec((bm, kd), lambda i, j: (i, 0)),
                  pl.BlockSpec((None, kd, bn), lambda i, j: (layer, 0, j))],
        out_specs=pl.BlockSpec((bm, bn), lambda i, j: (i, j)),
        out_shape=jax.ShapeDtypeStruct((m, n), F32),
        compiler_params=_params(("parallel", "arbitrary"), 52),
        name="in_proj",
    )(h, w_stack)


def _attn_prep_body(q_ref, k_ref, v_ref, qg_ref, kg_ref, qa_ref, qt_ref, kp_ref, vt_ref):
    rows = q_ref.shape[0]
    lane = lax.broadcasted_iota(jnp.int32, (1, V_DIM), 1)
    lo = lane < QK_DIM

    def group_norm(t, gain):
        t2 = t * t
        s_lo = jnp.sum(jnp.where(lo, t2, 0.0), axis=-1, keepdims=True)
        s_hi = jnp.sum(jnp.where(lo, 0.0, t2), axis=-1, keepdims=True)
        ms = jnp.where(lo, s_lo, s_hi) * (1.0 / QK_DIM)
        return (t * lax.rsqrt(ms + EPS)) * gain

    qn = group_norm(q_ref[...], qg_ref[...]) * (QK_DIM ** -0.5 * LOG2E)
    qa = qa_ref[...]
    qt_ref[0] = jnp.where(lo, qn, qa).T.astype(BF16)
    qt_ref[1] = jnp.where(lo, pltpu.roll(qn, QK_DIM, axis=1), qa).T.astype(BF16)

    kn = group_norm(k_ref[...], kg_ref[...])
    pos = lax.broadcasted_iota(jnp.int32, (rows, V_DIM), 0)
    pos_lo = (pos & 255).astype(F32)
    pos_hi = (pos - (pos & 255)).astype(F32)
    ka = jnp.where((lane >= QK_DIM) & (lane < QK_DIM + 3), pos_lo,
                   jnp.where((lane >= QK_DIM + 3) & (lane < QK_DIM + ALIBI_COLS), pos_hi, 0.0))
    kp_ref[0] = jnp.where(lo, kn, ka).astype(BF16)
    kp_ref[1] = jnp.where(lo, pltpu.roll(kn, QK_DIM, axis=1), ka).astype(BF16)

    vt_ref[0:V_DIM, :] = v_ref[...].T.astype(BF16)
    vt_ref[V_DIM:, :] = jnp.ones((ONES_ROWS, rows), BF16)


def _attn_prep(z, q_gain, k_gain, q_aug, n_heads, blk):
    bsz, seq, _ = z.shape
    nb = seq // blk
    gain_spec = pl.BlockSpec((1, V_DIM), lambda b, h, s: (0, 0))
    return pl.pallas_call(
        _attn_prep_body,
        grid=(bsz, n_heads, nb),
        in_specs=[pl.BlockSpec((None, blk, V_DIM), lambda b, h, s: (b, s, h)),
                  pl.BlockSpec((None, blk, V_DIM), lambda b, h, s: (b, s, n_heads + h)),
                  pl.BlockSpec((None, blk, V_DIM), lambda b, h, s: (b, s, 2 * n_heads + h)),
                  gain_spec, gain_spec,
                  pl.BlockSpec((None, 1, V_DIM), lambda b, h, s: (h, 0, 0))],
        out_specs=[pl.BlockSpec((None, None, 2, V_DIM, blk), lambda b, h, s: (b, h, 0, 0, s)),
                   pl.BlockSpec((None, None, 2, blk, V_DIM), lambda b, h, s: (b, h, 0, s, 0)),
                   pl.BlockSpec((None, None, None, VT_ROWS, blk), lambda b, h, s: (b, h, s, 0, 0))],
        out_shape=[jax.ShapeDtypeStruct((bsz, n_heads, 2, V_DIM, seq), BF16),
                   jax.ShapeDtypeStruct((bsz, n_heads, 2, seq, V_DIM), BF16),
                   jax.ShapeDtypeStruct((bsz, n_heads, nb, VT_ROWS, blk), BF16)],
        compiler_params=_params(("parallel", "parallel", "parallel"), 32),
        name="attn_prep",
    )(z, z, z, jnp.tile(q_gain, 2).reshape(1, V_DIM), jnp.tile(k_gain, 2).reshape(1, V_DIM), q_aug)


def _attn_body(scal_ref, sl_ref, qt_ref, k_ref, vt_ref, sg_ref, o_ref,
               m_s, acc_s, st_a, st_b, *, blk):
    hd = pl.program_id(1)
    qi = pl.program_id(2)
    sl = sl_ref[hd]
    m_s[...] = jnp.full_like(m_s, NEG_BIG)
    acc_s[...] = jnp.zeros_like(acc_s)
    st_bufs = (st_a, st_b)

    def scores(j, dst):
        r = pl.multiple_of(j * blk, blk)
        for mi in range(2):
            st_bufs[dst][mi] = jnp.dot(k_ref[mi, pl.ds(r, blk), :], qt_ref[mi],
                                       preferred_element_type=F32)

    def absorb(j, src, diagonal):
        off = sl * ((j - qi) * blk).astype(F32)
        for mi in range(2):
            st = st_bufs[src][mi]
            if diagonal:
                key = lax.broadcasted_iota(jnp.int32, st.shape, 0)
                qry = lax.broadcasted_iota(jnp.int32, st.shape, 1)
                st = jnp.where(key <= qry, st, -jnp.inf)
            m_old = m_s[mi]
            m_new = jnp.maximum(m_old, jnp.max(st, axis=0, keepdims=True) + off)
            alpha = jnp.exp2(m_old - m_new)
            pt = jnp.exp2(st - (m_new - off)).astype(BF16)
            acc_s[mi] = alpha * acc_s[mi] + jnp.dot(vt_ref[j], pt, preferred_element_type=F32)
            m_s[mi] = m_new

    scores(0, 0)

    def pair(t, carry):
        j = 2 * t
        scores(j + 1, 1)
        absorb(j, 0, False)
        scores(j + 2, 0)
        absorb(j + 1, 1, False)
        return carry
    lax.fori_loop(0, qi // 2, pair, 0)

    @pl.when(qi % 2 == 0)
    def _():
        absorb(qi, 0, True)

    @pl.when(qi % 2 == 1)
    def _():
        scores(qi, 1)
        absorb(qi - 1, 0, False)
        absorb(qi, 1, True)

    lam = scal_ref[0]
    a0 = acc_s[0]
    a1 = acc_s[1]
    o = a0[:V_DIM] / a0[V_DIM:V_DIM + 1] - lam * (a1[:V_DIM] / a1[V_DIM:V_DIM + 1])
    ms = jnp.mean(o * o, axis=0, keepdims=True)
    y = ((o * lax.rsqrt(ms + EPS)) * sg_ref[...]) * scal_ref[1]
    o_ref[...] = y.T.astype(o_ref.dtype)


def _bf16_part(x):
    bits = lax.bitcast_convert_type(x, jnp.uint32) & jnp.uint32(0xFFFF0000)
    return lax.bitcast_convert_type(bits, F32)


def _alibi_columns(n_heads):
    slopes = jnp.exp2(-8.0 * jnp.arange(1, n_heads + 1, dtype=F32) / n_heads)
    sl = slopes * LOG2E
    a1 = _bf16_part(sl)
    a2 = _bf16_part(sl - a1)
    a3 = _bf16_part(sl - a1 - a2)
    pieces = jnp.stack([a1, a2, a3, a1, a2, a3], axis=1)
    q_aug = jnp.zeros((n_heads, 1, V_DIM), F32).at[:, 0, QK_DIM:QK_DIM + ALIBI_COLS].set(pieces)
    return a1 + a2 + a3, q_aug


def _diff_attention(z, scal, q_gain, k_gain, subln_gain, n_heads):
    bsz, seq, _ = z.shape
    blk = _tile(seq, ATTN_BLOCK)
    sl, q_aug = _alibi_columns(n_heads)
    qt, kp, vt = _attn_prep(z, q_gain, k_gain, q_aug, n_heads, blk)
    smem = pl.BlockSpec(memory_space=pltpu.SMEM)
    return pl.pallas_call(
        functools.partial(_attn_body, blk=blk),
        grid=(bsz, n_heads, seq // blk),
        in_specs=[smem, smem,
                  pl.BlockSpec((None, None, 2, V_DIM, blk), lambda b, h, i: (b, h, 0, 0, i)),
                  pl.BlockSpec((None, None, 2, seq, V_DIM), lambda b, h, i: (b, h, 0, 0, 0)),
                  pl.BlockSpec((None, None, seq // blk, VT_ROWS, blk), lambda b, h, i: (b, h, 0, 0, 0)),
                  pl.BlockSpec((V_DIM, 1), lambda b, h, i: (0, 0))],
        out_specs=pl.BlockSpec((None, blk, V_DIM), lambda b, h, i: (b, i, h)),
        out_shape=jax.ShapeDtypeStruct((bsz, seq, n_heads * V_DIM), BF16),
        scratch_shapes=[pltpu.VMEM((2, 1, blk), F32),
                        pltpu.VMEM((2, VT_ROWS, blk), F32),
                        pltpu.VMEM((2, blk, blk), F32),
                        pltpu.VMEM((2, blk, blk), F32)],
        compiler_params=_params(("parallel", "parallel", "arbitrary"), 48),
        name="diff_attn",
    )(scal, sl, qt, kp, vt, subln_gain.reshape(V_DIM, 1))


def _conv_body(b_ref, c_ref, u_ref, w_ref, g_ref, o_ref, tail_s):
    si = pl.program_id(2)
    rows, cols = o_ref.shape

    @pl.when(si == 0)
    def _():
        tail_s[...] = jnp.zeros_like(tail_s)

    cu = c_ref[...] * u_ref[...]
    row = lax.broadcasted_iota(jnp.int32, (rows, cols), 0)
    tail = tail_s[...]
    prev1 = jnp.where(row == 0, tail[7:8, :], pltpu.roll(cu, 1, axis=0))
    prev2 = jnp.where(row == 0, tail[6:7, :],
                      jnp.where(row == 1, tail[7:8, :], pltpu.roll(cu, 2, axis=0)))
    acc = prev2 * w_ref[0:1, :]
    acc = acc + prev1 * w_ref[1:2, :]
    acc = acc + cu * w_ref[2:3, :]
    y = b_ref[...] * acc
    tail_s[...] = cu[rows - 8:, :]
    for gi in range(cols // CONV_GROUP_DIM):
        sl = slice(gi * CONV_GROUP_DIM, (gi + 1) * CONV_GROUP_DIM)
        yg = y[:, sl]
        ms = jnp.mean(yg * yg, axis=-1, keepdims=True)
        o_ref[:, sl] = ((yg * lax.rsqrt(ms + EPS)) * g_ref[...]).astype(o_ref.dtype)


def _gated_conv(z, col0, width, conv_w, group_gain):
    bsz, seq, _ = z.shape
    cb = _tile(width, 512)
    bs = _tile(seq, 512)
    nb = width // cb
    base = col0 // cb
    zspec = lambda slab: pl.BlockSpec(
        (None, bs, cb), lambda b, c, s: (b, s, base + slab * nb + c))
    return pl.pallas_call(
        _conv_body,
        grid=(bsz, nb, seq // bs),
        in_specs=[zspec(0), zspec(1), zspec(2),
                  pl.BlockSpec((CONV_K, cb), lambda b, c, s: (0, c)),
                  pl.BlockSpec((1, CONV_GROUP_DIM), lambda b, c, s: (0, 0))],
        out_specs=pl.BlockSpec((None, bs, cb), lambda b, c, s: (b, s, c)),
        out_shape=jax.ShapeDtypeStruct((bsz, seq, width), BF16),
        scratch_shapes=[pltpu.VMEM((8, cb), F32)],
        compiler_params=_params(("parallel", "parallel", "arbitrary"), 32),
        name="gated_conv",
    )(z, z, z, conv_w, group_gain.reshape(1, CONV_GROUP_DIM))


def _out_proj_body(a_ref, y_ref, wa_ref, wc_ref, x_ref, o_ref):
    mix = (jnp.dot(a_ref[...], wa_ref[...].astype(BF16), preferred_element_type=F32)
           + jnp.dot(y_ref[...], wc_ref[...].astype(BF16), preferred_element_type=F32))
    o_ref[...] = x_ref[...] + mix


def _out_proj(attn, conv, w_stack, layer, x):
    m, d = x.shape
    bm = _tile(m, 1024)
    bn = _tile(d, 512)
    ka, kc = attn.shape[1], conv.shape[1]
    assert ka == kc, "row blocks of the output projection are indexed in units of one head group"
    return pl.pallas_call(
        _out_proj_body,
        grid=(m // bm, d // bn),
        in_specs=[pl.BlockSpec((bm, ka), lambda i, j: (i, 0)),
                  pl.BlockSpec((bm, kc), lambda i, j: (i, 0)),
                  pl.BlockSpec((None, ka, bn), lambda i, j: (layer, 0, j)),
                  pl.BlockSpec((None, kc, bn), lambda i, j: (layer, 1, j)),
                  pl.BlockSpec((bm, bn), lambda i, j: (i, j))],
        out_specs=pl.BlockSpec((bm, bn), lambda i, j: (i, j)),
        out_shape=jax.ShapeDtypeStruct((m, d), F32),
        compiler_params=_params(("parallel", "arbitrary"), 52),
        name="out_proj",
    )(attn, conv, w_stack, w_stack, x)


def _ple_body(hg_ref, wg_ref, p_ref, wp_ref, pg_ref, x_ref, o_ref, e_s, inv_s):
    j = pl.program_id(1)
    n_tiles, _, bn = e_s.shape

    @pl.when(j == 0)
    def _():
        pb = p_ref[...].astype(BF16)
        ssq = jnp.zeros(inv_s.shape, F32)
        for t in range(n_tiles):
            e = jnp.dot(pb, wp_ref[:, t * bn:(t + 1) * bn].astype(BF16),
                        preferred_element_type=F32)
            e_s[t] = e
            ssq = ssq + jnp.sum(e * e, axis=-1, keepdims=True)
        inv_s[...] = lax.rsqrt(ssq / (n_tiles * bn) + EPS)

    gate = jax.nn.sigmoid(jnp.dot(hg_ref[...], wg_ref[...], preferred_element_type=F32))
    e = (e_s[j] * inv_s[...]) * pg_ref[...]
    o_ref[...] = x_ref[...] + gate * e


def _ple(hg, w_gate, p, w_proj_stack, layer, post_gain, x):
    m, d = x.shape
    pd = p.shape[1]
    bm = _tile(m, 512)
    bn = _tile(d, 512)
    return pl.pallas_call(
        _ple_body,
        grid=(m // bm, d // bn),
        in_specs=[pl.BlockSpec((bm, d), lambda i, j: (i, 0)),
                  pl.BlockSpec((d, bn), lambda i, j: (0, j)),
                  pl.BlockSpec((bm, pd), lambda i, j: (i, 0)),
                  pl.BlockSpec((None, pd, d), lambda i, j: (layer, 0, 0)),
                  pl.BlockSpec((1, bn), lambda i, j: (0, j)),
                  pl.BlockSpec((bm, bn), lambda i, j: (i, j))],
        out_specs=pl.BlockSpec((bm, bn), lambda i, j: (i, j)),
        out_shape=jax.ShapeDtypeStruct((m, d), F32),
        scratch_shapes=[pltpu.VMEM((d // bn, bm, bn), F32),
                        pltpu.VMEM((bm, 1), F32)],
        compiler_params=_params(("parallel", "arbitrary"), 56),
        name="ple",
    )(hg, w_gate, p, w_proj_stack, post_gain.reshape(1, d), x)


def kernel(x, p, ffn1_norm, ffn1_w_gate, ffn1_w_up, ffn1_w_down, mix_norm, w_in, q_norm, k_norm, lambda_q1, lambda_k1, lambda_q2, lambda_k2, attn_subln, conv_w, conv_norm, w_out, ffn2_norm, ffn2_w_gate, ffn2_w_up, ffn2_w_down, ple_w_proj, ple_post_norm, ple_gate_norm, ple_w_gate):
    bsz, seq, d = x.shape
    depth = p.shape[0]
    m = bsz * seq
    attn_width = w_out.shape[1] - conv_w.shape[2]
    conv_width = conv_w.shape[2]
    n_heads = attn_width // V_DIM
    assert w_in.shape[2] == 3 * attn_width + 3 * conv_width

    xf = x.reshape(m, d)
    for i in range(depth):
        lam_init = 0.8 - 0.6 * math.exp(-0.3 * i)
        lam = (jnp.exp(jnp.sum(lambda_q1[i] * lambda_k1[i]))
               - jnp.exp(jnp.sum(lambda_q2[i] * lambda_k2[i])) + lam_init)
        scal = jnp.stack([lam, jnp.asarray(1.0 - lam_init, F32)]).astype(F32)

        xf = _ffn(xf, ffn1_norm[i], ffn1_w_gate, ffn1_w_up, ffn1_w_down, i)

        h = _rmsnorm(xf, mix_norm[i])
        z = _in_proj(h, w_in, i).reshape(bsz, seq, -1)
        attn = _diff_attention(z, scal, q_norm[i], k_norm[i], attn_subln[i], n_heads)
        conv = _gated_conv(z, 3 * attn_width, conv_width, conv_w[i], conv_norm[i])
        xf = _out_proj(attn.reshape(m, attn_width), conv.reshape(m, conv_width), w_out, i, xf)

        xf = _ffn(xf, ffn2_norm[i], ffn2_w_gate, ffn2_w_up, ffn2_w_down, i)

        hg = _rmsnorm(xf, ple_gate_norm[i])
        xf = _ple(hg, _cast_layer(ple_w_gate, i), p[i].reshape(m, -1), ple_w_proj, i,
                  ple_post_norm[i], xf)
    return xf.reshape(bsz, seq, d)
```

```python
import functools
import math

import jax
import jax.numpy as jnp
from jax import lax
from jax.experimental import pallas as pl
from jax.experimental.pallas import tpu as pltpu

F32 = jnp.float32
BF16 = jnp.bfloat16

EPS = 1e-6
HALF_STEP = 0.5
QK_DIM = 64
V_DIM = 2 * QK_DIM
CONV_GROUP_DIM = 128
CONV_K = 3
PLE_DIM = 256
NEG_BIG = -1e30
LOG2E = 1.4426950408889634
ALIBI_COLS = 6
ONES_ROWS = 16
VT_ROWS = V_DIM + ONES_ROWS
ATTN_BLOCK = 512
ATTN_HEADS_PER_STEP = 2

MIB = 1024 * 1024


def _tile(dim, pref):
    t = min(dim, pref)
    while dim % t:
        t //= 2
    return t


def _params(semantics, vmem_mib):
    return pltpu.CompilerParams(dimension_semantics=semantics,
                                vmem_limit_bytes=vmem_mib * MIB)


def _norm_body(x_ref, g_ref, o_ref):
    x = x_ref[...]
    ms = jnp.mean(x * x, axis=-1, keepdims=True)
    o_ref[...] = ((x * lax.rsqrt(ms + EPS)) * g_ref[...]).astype(o_ref.dtype)


def _rmsnorm(x, gain):
    m, d = x.shape
    bm = _tile(m, 256)
    return pl.pallas_call(
        _norm_body,
        grid=(m // bm,),
        in_specs=[pl.BlockSpec((bm, d), lambda i: (i, 0)),
                  pl.BlockSpec((1, d), lambda i: (0, 0))],
        out_specs=pl.BlockSpec((bm, d), lambda i: (i, 0)),
        out_shape=jax.ShapeDtypeStruct((m, d), BF16),
        compiler_params=_params(("parallel",), 32),
        name="rmsnorm",
    )(x, gain.reshape(1, d))


def _cast_body(w_ref, o_ref):
    o_ref[...] = w_ref[...].astype(o_ref.dtype)


def _cast_layer(w_stack, layer):
    _, r, c = w_stack.shape
    br = _tile(r, 256)
    return pl.pallas_call(
        _cast_body,
        grid=(r // br,),
        in_specs=[pl.BlockSpec((None, br, c), lambda i: (layer, i, 0))],
        out_specs=pl.BlockSpec((br, c), lambda i: (i, 0)),
        out_shape=jax.ShapeDtypeStruct((r, c), BF16),
        compiler_params=_params(("parallel",), 40),
        name="cast_bf16",
    )(w_stack)


def _gate_up_body(h_ref, wg_ref, wu_ref, a_ref):
    h = h_ref[...]
    g = jnp.dot(h, wg_ref[...].astype(BF16), preferred_element_type=F32)
    u = jnp.dot(h, wu_ref[...].astype(BF16), preferred_element_type=F32)
    a_ref[...] = ((g * jax.nn.sigmoid(g)) * u).astype(a_ref.dtype)


def _gate_up(h, wg_stack, wu_stack, layer):
    m, d = h.shape
    f = wg_stack.shape[2]
    bm = _tile(m, 1024)
    bf = _tile(f, 256)
    wspec = pl.BlockSpec((None, d, bf), lambda i, k: (layer, 0, k))
    return pl.pallas_call(
        _gate_up_body,
        grid=(m // bm, f // bf),
        in_specs=[pl.BlockSpec((bm, d), lambda i, k: (i, 0)), wspec, wspec],
        out_specs=pl.BlockSpec((bm, bf), lambda i, k: (i, k)),
        out_shape=jax.ShapeDtypeStruct((m, f), BF16),
        compiler_params=_params(("parallel", "arbitrary"), 48),
        name="ffn_gate_up",
    )(h, wg_stack, wu_stack)


def _down_body(a_ref, wd_ref, x_ref, o_ref):
    o_ref[...] = x_ref[...] + HALF_STEP * jnp.dot(a_ref[...], wd_ref[...],
                                                  preferred_element_type=F32)


def _down(a, wd, x):
    m, d = x.shape
    f = a.shape[1]
    bm = _tile(m, 512)
    bn = _tile(d, 512)
    return pl.pallas_call(
        _down_body,
        grid=(m // bm, d // bn),
        in_specs=[pl.BlockSpec((bm, f), lambda i, j: (i, 0)),
                  pl.BlockSpec((f, bn), lambda i, j: (0, j)),
                  pl.BlockSpec((bm, bn), lambda i, j: (i, j))],
        out_specs=pl.BlockSpec((bm, bn), lambda i, j: (i, j)),
        out_shape=jax.ShapeDtypeStruct((m, d), F32),
        compiler_params=_params(("parallel", "arbitrary"), 56),
        name="ffn_down",
    )(a, wd, x)


def _ffn(x, gain, wg_stack, wu_stack, wd_stack, layer):
    a = _gate_up(_rmsnorm(x, gain), wg_stack, wu_stack, layer)
    return _down(a, _cast_layer(wd_stack, layer), x)


def _in_proj_body(h_ref, w_ref, o_ref):
    o_ref[...] = jnp.dot(h_ref[...], w_ref[...].astype(BF16), preferred_element_type=F32)


def _in_proj(h, w_stack, layer):
    m, kd = h.shape
    n = w_stack.shape[2]
    bm = _tile(m, 1024)
    bn = _tile(n, 512)
    return pl.pallas_call(
        _in_proj_body,
        grid=(m // bm, n // bn),
        in_specs=[pl.BlockSpec((bm, kd), lambda i, j: (i, 0)),
                  pl.BlockSpec((None, kd, bn), lambda i, j: (layer, 0, j))],
        out_specs=pl.BlockSpec((bm, bn), lambda i, j: (i, j)),
        out_shape=jax.ShapeDtypeStruct((m, n), F32),
        compiler_params=_params(("parallel", "arbitrary"), 52),
        name="in_proj",
    )(h, w_stack)


def _attn_prep_body(q_ref, k_ref, v_ref, qg_ref, kg_ref, qa_ref, qt_ref, kp_ref, vt_ref):
    rows = q_ref.shape[0]
    lane = lax.broadcasted_iota(jnp.int32, (1, V_DIM), 1)
    lo = lane < QK_DIM

    def group_norm(t, gain):
        t2 = t * t
        s_lo = jnp.sum(jnp.where(lo, t2, 0.0), axis=-1, keepdims=True)
        s_hi = jnp.sum(jnp.where(lo, 0.0, t2), axis=-1, keepdims=True)
        ms = jnp.where(lo, s_lo, s_hi) * (1.0 / QK_DIM)
        return (t * lax.rsqrt(ms + EPS)) * gain

    qn = group_norm(q_ref[...], qg_ref[...]) * (QK_DIM ** -0.5 * LOG2E)
    qa = qa_ref[...]
    qt_ref[0] = jnp.where(lo, qn, qa).T.astype(BF16)
    qt_ref[1] = jnp.where(lo, pltpu.roll(qn, QK_DIM, axis=1), qa).T.astype(BF16)

    kn = group_norm(k_ref[...], kg_ref[...])
    pos = lax.broadcasted_iota(jnp.int32, (rows, V_DIM), 0)
    pos_lo = (pos & 255).astype(F32)
    pos_hi = (pos - (pos & 255)).astype(F32)
    ka = jnp.where((lane >= QK_DIM) & (lane < QK_DIM + 3), pos_lo,
                   jnp.where((lane >= QK_DIM + 3) & (lane < QK_DIM + ALIBI_COLS), pos_hi, 0.0))
    kp_ref[0] = jnp.where(lo, kn, ka).astype(BF16)
    kp_ref[1] = jnp.where(lo, pltpu.roll(kn, QK_DIM, axis=1), ka).astype(BF16)

    vt_ref[0:V_DIM, :] = v_ref[...].T.astype(BF16)
    vt_ref[V_DIM:, :] = jnp.ones((ONES_ROWS, rows), BF16)


def _attn_prep(z, q_gain, k_gain, q_aug, n_heads, blk):
    bsz, seq, _ = z.shape
    nb = seq // blk
    gain_spec = pl.BlockSpec((1, V_DIM), lambda b, h, s: (0, 0))
    return pl.pallas_call(
        _attn_prep_body,
        grid=(bsz, n_heads, nb),
        in_specs=[pl.BlockSpec((None, blk, V_DIM), lambda b, h, s: (b, s, h)),
                  pl.BlockSpec((None, blk, V_DIM), lambda b, h, s: (b, s, n_heads + h)),
                  pl.BlockSpec((None, blk, V_DIM), lambda b, h, s: (b, s, 2 * n_heads + h)),
                  gain_spec, gain_spec,
                  pl.BlockSpec((None, 1, V_DIM), lambda b, h, s: (h, 0, 0))],
        out_specs=[pl.BlockSpec((None, None, 2, V_DIM, blk), lambda b, h, s: (b, h, 0, 0, s)),
                   pl.BlockSpec((None, None, 2, blk, V_DIM), lambda b, h, s: (b, h, 0, s, 0)),
                   pl.BlockSpec((None, None, None, VT_ROWS, blk), lambda b, h, s: (b, h, s, 0, 0))],
        out_shape=[jax.ShapeDtypeStruct((bsz, n_heads, 2, V_DIM, seq), BF16),
                   jax.ShapeDtypeStruct((bsz, n_heads, 2, seq, V_DIM), BF16),
                   jax.ShapeDtypeStruct((bsz, n_heads, nb, VT_ROWS, blk), BF16)],
        compiler_params=_params(("parallel", "parallel", "parallel"), 32),
        name="attn_prep",
    )(z, z, z, jnp.tile(q_gain, 2).reshape(1, V_DIM), jnp.tile(k_gain, 2).reshape(1, V_DIM), q_aug)


def _attn_body(scal_ref, sl_ref, qt_ref, k_ref, vt_ref, sg_ref, o_ref,
               m_s, acc_s, st_a, st_b, *, blk, hps):
    hg = pl.program_id(1)
    qi = pl.program_id(2)
    n_chains = 2 * hps
    m_s[...] = jnp.full_like(m_s, NEG_BIG)
    acc_s[...] = jnp.zeros_like(acc_s)
    st_bufs = (st_a, st_b)

    def scores(j, dst):
        r = pl.multiple_of(j * blk, blk)
        for c in range(n_chains):
            st_bufs[dst][c] = jnp.dot(k_ref[c // 2, c % 2, pl.ds(r, blk), :], qt_ref[c // 2, c % 2],
                                      preferred_element_type=F32)

    def absorb(j, src, diagonal):
        for c in range(n_chains):
            off = sl_ref[hg * hps + c // 2] * ((j - qi) * blk).astype(F32)
            st = st_bufs[src][c]
            if diagonal:
                key = lax.broadcasted_iota(jnp.int32, st.shape, 0)
                qry = lax.broadcasted_iota(jnp.int32, st.shape, 1)
                st = jnp.where(key <= qry, st, -jnp.inf)
            m_old = m_s[c]
            m_new = jnp.maximum(m_old, jnp.max(st, axis=0, keepdims=True) + off)
            alpha = jnp.exp2(m_old - m_new)
            pt = jnp.exp2(st - (m_new - off)).astype(BF16)
            acc_s[c] = alpha * acc_s[c] + jnp.dot(vt_ref[c // 2, j], pt,
                                                  preferred_element_type=F32)
            m_s[c] = m_new

    scores(0, 0)

    def pair(t, carry):
        j = 2 * t
        scores(j + 1, 1)
        absorb(j, 0, False)
        scores(j + 2, 0)
        absorb(j + 1, 1, False)
        return carry
    lax.fori_loop(0, qi // 2, pair, 0)

    @pl.when(qi % 2 == 0)
    def _():
        absorb(qi, 0, True)

    @pl.when(qi % 2 == 1)
    def _():
        scores(qi, 1)
        absorb(qi - 1, 0, False)
        absorb(qi, 1, True)

    lam = scal_ref[0]
    for hh in range(hps):
        a0 = acc_s[2 * hh]
        a1 = acc_s[2 * hh + 1]
        o = a0[:V_DIM] / a0[V_DIM:V_DIM + 1] - lam * (a1[:V_DIM] / a1[V_DIM:V_DIM + 1])
        ms = jnp.mean(o * o, axis=0, keepdims=True)
        y = ((o * lax.rsqrt(ms + EPS)) * sg_ref[...]) * scal_ref[1]
        o_ref[:, hh * V_DIM:(hh + 1) * V_DIM] = y.T.astype(o_ref.dtype)


def _bf16_part(x):
    bits = lax.bitcast_convert_type(x, jnp.uint32) & jnp.uint32(0xFFFF0000)
    return lax.bitcast_convert_type(bits, F32)


def _alibi_columns(n_heads):
    slopes = jnp.exp2(-8.0 * jnp.arange(1, n_heads + 1, dtype=F32) / n_heads)
    sl = slopes * LOG2E
    a1 = _bf16_part(sl)
    a2 = _bf16_part(sl - a1)
    a3 = _bf16_part(sl - a1 - a2)
    pieces = jnp.stack([a1, a2, a3, a1, a2, a3], axis=1)
    q_aug = jnp.zeros((n_heads, 1, V_DIM), F32).at[:, 0, QK_DIM:QK_DIM + ALIBI_COLS].set(pieces)
    return a1 + a2 + a3, q_aug


def _diff_attention(z, scal, q_gain, k_gain, subln_gain, n_heads):
    bsz, seq, _ = z.shape
    blk = _tile(seq, ATTN_BLOCK)
    sl, q_aug = _alibi_columns(n_heads)
    qt, kp, vt = _attn_prep(z, q_gain, k_gain, q_aug, n_heads, blk)
    hps = ATTN_HEADS_PER_STEP if n_heads % ATTN_HEADS_PER_STEP == 0 else 1
    n_chains = 2 * hps
    smem = pl.BlockSpec(memory_space=pltpu.SMEM)
    return pl.pallas_call(
        functools.partial(_attn_body, blk=blk, hps=hps),
        grid=(bsz, n_heads // hps, seq // blk),
        in_specs=[smem, smem,
                  pl.BlockSpec((None, hps, 2, V_DIM, blk), lambda b, h, i: (b, h, 0, 0, i)),
                  pl.BlockSpec((None, hps, 2, seq, V_DIM), lambda b, h, i: (b, h, 0, 0, 0)),
                  pl.BlockSpec((None, hps, seq // blk, VT_ROWS, blk), lambda b, h, i: (b, h, 0, 0, 0)),
                  pl.BlockSpec((V_DIM, 1), lambda b, h, i: (0, 0))],
        out_specs=pl.BlockSpec((None, blk, hps * V_DIM), lambda b, h, i: (b, i, h)),
        out_shape=jax.ShapeDtypeStruct((bsz, seq, n_heads * V_DIM), BF16),
        scratch_shapes=[pltpu.VMEM((n_chains, 1, blk), F32),
                        pltpu.VMEM((n_chains, VT_ROWS, blk), F32),
                        pltpu.VMEM((n_chains, blk, blk), F32),
                        pltpu.VMEM((n_chains, blk, blk), F32)],
        compiler_params=_params(("parallel", "parallel", "arbitrary"), 56),
        name="diff_attn",
    )(scal, sl, qt, kp, vt, subln_gain.reshape(V_DIM, 1))


def _conv_body(b_ref, c_ref, u_ref, w_ref, g_ref, o_ref, tail_s):
    si = pl.program_id(2)
    rows, cols = o_ref.shape

    @pl.when(si == 0)
    def _():
        tail_s[...] = jnp.zeros_like(tail_s)

    cu = c_ref[...] * u_ref[...]
    row = lax.broadcasted_iota(jnp.int32, (rows, cols), 0)
    tail = tail_s[...]
    prev1 = jnp.where(row == 0, tail[7:8, :], pltpu.roll(cu, 1, axis=0))
    prev2 = jnp.where(row == 0, tail[6:7, :],
                      jnp.where(row == 1, tail[7:8, :], pltpu.roll(cu, 2, axis=0)))
    acc = prev2 * w_ref[0:1, :]
    acc = acc + prev1 * w_ref[1:2, :]
    acc = acc + cu * w_ref[2:3, :]
    y = b_ref[...] * acc
    tail_s[...] = cu[rows - 8:, :]
    for gi in range(cols // CONV_GROUP_DIM):
        sl = slice(gi * CONV_GROUP_DIM, (gi + 1) * CONV_GROUP_DIM)
        yg = y[:, sl]
        ms = jnp.mean(yg * yg, axis=-1, keepdims=True)
        o_ref[:, sl] = ((yg * lax.rsqrt(ms + EPS)) * g_ref[...]).astype(o_ref.dtype)


def _gated_conv(z, col0, width, conv_w, group_gain):
    bsz, seq, _ = z.shape
    cb = _tile(width, 512)
    bs = _tile(seq, 512)
    nb = width // cb
    base = col0 // cb
    zspec = lambda slab: pl.BlockSpec(
        (None, bs, cb), lambda b, c, s: (b, s, base + slab * nb + c))
    return pl.pallas_call(
        _conv_body,
        grid=(bsz, nb, seq // bs),
        in_specs=[zspec(0), zspec(1), zspec(2),
                  pl.BlockSpec((CONV_K, cb), lambda b, c, s: (0, c)),
                  pl.BlockSpec((1, CONV_GROUP_DIM), lambda b, c, s: (0, 0))],
        out_specs=pl.BlockSpec((None, bs, cb), lambda b, c, s: (b, s, c)),
        out_shape=jax.ShapeDtypeStruct((bsz, seq, width), BF16),
        scratch_shapes=[pltpu.VMEM((8, cb), F32)],
        compiler_params=_params(("parallel", "parallel", "arbitrary"), 32),
        name="gated_conv",
    )(z, z, z, conv_w, group_gain.reshape(1, CONV_GROUP_DIM))


def _out_proj_body(a_ref, y_ref, wa_ref, wc_ref, x_ref, o_ref):
    mix = (jnp.dot(a_ref[...], wa_ref[...].astype(BF16), preferred_element_type=F32)
           + jnp.dot(y_ref[...], wc_ref[...].astype(BF16), preferred_element_type=F32))
    o_ref[...] = x_ref[...] + mix


def _out_proj(attn, conv, w_stack, layer, x):
    m, d = x.shape
    bm = _tile(m, 1024)
    bn = _tile(d, 512)
    ka, kc = attn.shape[1], conv.shape[1]
    assert ka == kc, "row blocks of the output projection are indexed in units of one head group"
    return pl.pallas_call(
        _out_proj_body,
        grid=(m // bm, d // bn),
        in_specs=[pl.BlockSpec((bm, ka), lambda i, j: (i, 0)),
                  pl.BlockSpec((bm, kc), lambda i, j: (i, 0)),
                  pl.BlockSpec((None, ka, bn), lambda i, j: (layer, 0, j)),
                  pl.BlockSpec((None, kc, bn), lambda i, j: (layer, 1, j)),
                  pl.BlockSpec((bm, bn), lambda i, j: (i, j))],
        out_specs=pl.BlockSpec((bm, bn), lambda i, j: (i, j)),
        out_shape=jax.ShapeDtypeStruct((m, d), F32),
        compiler_params=_params(("parallel", "arbitrary"), 52),
        name="out_proj",
    )(attn, conv, w_stack, w_stack, x)


def _ple_body(hg_ref, wg_ref, p_ref, wp_ref, pg_ref, x_ref, o_ref, e_s, inv_s):
    j = pl.program_id(1)
    n_tiles, _, bn = e_s.shape

    @pl.when(j == 0)
    def _():
        pb = p_ref[...].astype(BF16)
        ssq = jnp.zeros(inv_s.shape, F32)
        for t in range(n_tiles):
            e = jnp.dot(pb, wp_ref[:, t * bn:(t + 1) * bn].astype(BF16),
                        preferred_element_type=F32)
            e_s[t] = e
            ssq = ssq + jnp.sum(e * e, axis=-1, keepdims=True)
        inv_s[...] = lax.rsqrt(ssq / (n_tiles * bn) + EPS)

    gate = jax.nn.sigmoid(jnp.dot(hg_ref[...], wg_ref[...], preferred_element_type=F32))
    e = (e_s[j] * inv_s[...]) * pg_ref[...]
    o_ref[...] = x_ref[...] + gate * e


def _ple(hg, w_gate, p, w_proj_stack, layer, post_gain, x):
    m, d = x.shape
    pd = p.shape[1]
    bm = _tile(m, 512)
    bn = _tile(d, 1024)
    return pl.pallas_call(
        _ple_body,
        grid=(m // bm, d // bn),
        in_specs=[pl.BlockSpec((bm, d), lambda i, j: (i, 0)),
                  pl.BlockSpec((d, bn), lambda i, j: (0, j)),
                  pl.BlockSpec((bm, pd), lambda i, j: (i, 0)),
                  pl.BlockSpec((None, pd, d), lambda i, j: (layer, 0, 0)),
                  pl.BlockSpec((1, bn), lambda i, j: (0, j)),
                  pl.BlockSpec((bm, bn), lambda i, j: (i, j))],
        out_specs=pl.BlockSpec((bm, bn), lambda i, j: (i, j)),
        out_shape=jax.ShapeDtypeStruct((m, d), F32),
        scratch_shapes=[pltpu.VMEM((d // bn, bm, bn), F32),
                        pltpu.VMEM((bm, 1), F32)],
        compiler_params=_params(("parallel", "arbitrary"), 56),
        name="ple",
    )(hg, w_gate, p, w_proj_stack, post_gain.reshape(1, d), x)


def kernel(x, p, ffn1_norm, ffn1_w_gate, ffn1_w_up, ffn1_w_down, mix_norm, w_in, q_norm, k_norm, lambda_q1, lambda_k1, lambda_q2, lambda_k2, attn_subln, conv_w, conv_norm, w_out, ffn2_norm, ffn2_w_gate, ffn2_w_up, ffn2_w_down, ple_w_proj, ple_post_norm, ple_gate_norm, ple_w_gate):
    bsz, seq, d = x.shape
    depth = p.shape[0]
    m = bsz * seq
    attn_width = w_out.shape[1] - conv_w.shape[2]
    conv_width = conv_w.shape[2]
    n_heads = attn_width // V_DIM
    assert w_in.shape[2] == 3 * attn_width + 3 * conv_width

    xf = x.reshape(m, d)
    for i in range(depth):
        lam_init = 0.8 - 0.6 * math.exp(-0.3 * i)
        lam = (jnp.exp(jnp.sum(lambda_q1[i] * lambda_k1[i]))
               - jnp.exp(jnp.sum(lambda_q2[i] * lambda_k2[i])) + lam_init)
        scal = jnp.stack([lam, jnp.asarray(1.0 - lam_init, F32)]).astype(F32)

        xf = _ffn(xf, ffn1_norm[i], ffn1_w_gate, ffn1_w_up, ffn1_w_down, i)

        h = _rmsnorm(xf, mix_norm[i])
        z = _in_proj(h, w_in, i).reshape(bsz, seq, -1)
        attn = _diff_attention(z, scal, q_norm[i], k_norm[i], attn_subln[i], n_heads)
        conv = _gated_conv(z, 3 * attn_width, conv_width, conv_w[i], conv_norm[i])
        xf = _out_proj(attn.reshape(m, attn_width), conv.reshape(m, conv_width), w_out, i, xf)

        xf = _ffn(xf, ffn2_norm[i], ffn2_w_gate, ffn2_w_up, ffn2_w_down, i)

        hg = _rmsnorm(xf, ple_gate_norm[i])
        xf = _ple(hg, _cast_layer(ple_w_gate, i), p[i].reshape(m, -1), ple_w_proj, i,
                  ple_post_norm[i], xf)
    return xf.reshape(bsz, seq, d)
```

```python
import functools
import math

import jax
import jax.numpy as jnp
from jax import lax
from jax.experimental import pallas as pl
from jax.experimental.pallas import tpu as pltpu

F32 = jnp.float32
BF16 = jnp.bfloat16

EPS = 1e-6
HALF_STEP = 0.5
QK_DIM = 64
V_DIM = 2 * QK_DIM
CONV_GROUP_DIM = 128
CONV_K = 3
PLE_DIM = 256
NEG_BIG = -1e30
LOG2E = 1.4426950408889634
ALIBI_COLS = 6
ONES_ROWS = 16
VT_ROWS = V_DIM + ONES_ROWS
ATTN_BLOCK = 512
ATTN_HEADS_PER_STEP = 4
PREP_BLOCKS = 2

MIB = 1024 * 1024


def _tile(dim, pref):
    t = min(dim, pref)
    while dim % t:
        t //= 2
    return t


def _params(semantics, vmem_mib):
    return pltpu.CompilerParams(dimension_semantics=semantics,
                                vmem_limit_bytes=vmem_mib * MIB)


def _norm_body(x_ref, g_ref, o_ref):
    x = x_ref[...]
    ms = jnp.mean(x * x, axis=-1, keepdims=True)
    o_ref[...] = ((x * lax.rsqrt(ms + EPS)) * g_ref[...]).astype(o_ref.dtype)


def _rmsnorm(x, gain):
    m, d = x.shape
    bm = _tile(m, 256)
    return pl.pallas_call(
        _norm_body,
        grid=(m // bm,),
        in_specs=[pl.BlockSpec((bm, d), lambda i: (i, 0)),
                  pl.BlockSpec((1, d), lambda i: (0, 0))],
        out_specs=pl.BlockSpec((bm, d), lambda i: (i, 0)),
        out_shape=jax.ShapeDtypeStruct((m, d), BF16),
        compiler_params=_params(("parallel",), 32),
        name="rmsnorm",
    )(x, gain.reshape(1, d))


def _cast_body(w_ref, o_ref):
    o_ref[...] = w_ref[...].astype(o_ref.dtype)


def _cast_layer(w_stack, layer):
    _, r, c = w_stack.shape
    br = _tile(r, 256)
    return pl.pallas_call(
        _cast_body,
        grid=(r // br,),
        in_specs=[pl.BlockSpec((None, br, c), lambda i: (layer, i, 0))],
        out_specs=pl.BlockSpec((br, c), lambda i: (i, 0)),
        out_shape=jax.ShapeDtypeStruct((r, c), BF16),
        compiler_params=_params(("parallel",), 40),
        name="cast_bf16",
    )(w_stack)


def _gate_up_body(h_ref, wg_ref, wu_ref, a_ref):
    h = h_ref[...]
    g = jnp.dot(h, wg_ref[...].astype(BF16), preferred_element_type=F32)
    u = jnp.dot(h, wu_ref[...].astype(BF16), preferred_element_type=F32)
    a_ref[...] = ((g * jax.nn.sigmoid(g)) * u).astype(a_ref.dtype)


def _gate_up(h, wg_stack, wu_stack, layer):
    m, d = h.shape
    f = wg_stack.shape[2]
    bm = _tile(m, 1024)
    bf = _tile(f, 256)
    wspec = pl.BlockSpec((None, d, bf), lambda i, k: (layer, 0, k))
    return pl.pallas_call(
        _gate_up_body,
        grid=(m // bm, f // bf),
        in_specs=[pl.BlockSpec((bm, d), lambda i, k: (i, 0)), wspec, wspec],
        out_specs=pl.BlockSpec((bm, bf), lambda i, k: (i, k)),
        out_shape=jax.ShapeDtypeStruct((m, f), BF16),
        compiler_params=_params(("parallel", "arbitrary"), 48),
        name="ffn_gate_up",
    )(h, wg_stack, wu_stack)


def _down_body(a_ref, wd_ref, x_ref, o_ref):
    o_ref[...] = x_ref[...] + HALF_STEP * jnp.dot(a_ref[...], wd_ref[...],
                                                  preferred_element_type=F32)


def _down(a, wd, x):
    m, d = x.shape
    f = a.shape[1]
    bm = _tile(m, 512)
    bn = _tile(d, 512)
    return pl.pallas_call(
        _down_body,
        grid=(m // bm, d // bn),
        in_specs=[pl.BlockSpec((bm, f), lambda i, j: (i, 0)),
                  pl.BlockSpec((f, bn), lambda i, j: (0, j)),
                  pl.BlockSpec((bm, bn), lambda i, j: (i, j))],
        out_specs=pl.BlockSpec((bm, bn), lambda i, j: (i, j)),
        out_shape=jax.ShapeDtypeStruct((m, d), F32),
        compiler_params=_params(("parallel", "arbitrary"), 56),
        name="ffn_down",
    )(a, wd, x)


def _ffn(x, gain, wg_stack, wu_stack, wd_stack, layer):
    a = _gate_up(_rmsnorm(x, gain), wg_stack, wu_stack, layer)
    return _down(a, _cast_layer(wd_stack, layer), x)


def _in_proj_body(h_ref, w_ref, o_ref):
    o_ref[...] = jnp.dot(h_ref[...], w_ref[...].astype(BF16), preferred_element_type=F32)


def _in_proj(h, w_stack, layer):
    m, kd = h.shape
    n = w_stack.shape[2]
    bm = _tile(m, 1024)
    bn = _tile(n, 512)
    return pl.pallas_call(
        _in_proj_body,
        grid=(m // bm, n // bn),
        in_specs=[pl.BlockSpec((bm, kd), lambda i, j: (i, 0)),
                  pl.BlockSpec((None, kd, bn), lambda i, j: (layer, 0, j))],
        out_specs=pl.BlockSpec((bm, bn), lambda i, j: (i, j)),
        out_shape=jax.ShapeDtypeStruct((m, n), F32),
        compiler_params=_params(("parallel", "arbitrary"), 52),
        name="in_proj",
    )(h, w_stack)


def _attn_prep_body(q_ref, k_ref, v_ref, qg_ref, kg_ref, qa_ref, qt_ref, kp_ref, vt_ref, *, blk):
    rows = q_ref.shape[0]
    lane = lax.broadcasted_iota(jnp.int32, (1, V_DIM), 1)
    lo = lane < QK_DIM

    def group_norm(t, gain):
        t2 = t * t
        s_lo = jnp.sum(jnp.where(lo, t2, 0.0), axis=-1, keepdims=True)
        s_hi = jnp.sum(jnp.where(lo, 0.0, t2), axis=-1, keepdims=True)
        ms = jnp.where(lo, s_lo, s_hi) * (1.0 / QK_DIM)
        return (t * lax.rsqrt(ms + EPS)) * gain

    qn = group_norm(q_ref[...], qg_ref[...]) * (QK_DIM ** -0.5 * LOG2E)
    qa = qa_ref[...]
    qt_ref[0] = jnp.where(lo, qn, qa).T.astype(BF16)
    qt_ref[1] = jnp.where(lo, pltpu.roll(qn, QK_DIM, axis=1), qa).T.astype(BF16)

    kn = group_norm(k_ref[...], kg_ref[...])
    pos = lax.broadcasted_iota(jnp.int32, (rows, V_DIM), 0) & (blk - 1)
    pos_lo = (pos & 255).astype(F32)
    pos_hi = (pos - (pos & 255)).astype(F32)
    ka = jnp.where((lane >= QK_DIM) & (lane < QK_DIM + 3), pos_lo,
                   jnp.where((lane >= QK_DIM + 3) & (lane < QK_DIM + ALIBI_COLS), pos_hi, 0.0))
    kp_ref[0] = jnp.where(lo, kn, ka).astype(BF16)
    kp_ref[1] = jnp.where(lo, pltpu.roll(kn, QK_DIM, axis=1), ka).astype(BF16)

    for t in range(rows // blk):
        vt_ref[t, 0:V_DIM, :] = v_ref[t * blk:(t + 1) * blk, :].T.astype(BF16)
        vt_ref[t, V_DIM:, :] = jnp.ones((ONES_ROWS, blk), BF16)


def _attn_prep(z, q_gain, k_gain, q_aug, n_heads, blk):
    bsz, seq, _ = z.shape
    assert blk & (blk - 1) == 0
    nb = seq // blk
    per = PREP_BLOCKS if nb % PREP_BLOCKS == 0 else 1
    rows = per * blk
    gain_spec = pl.BlockSpec((1, V_DIM), lambda b, h, s: (0, 0))
    return pl.pallas_call(
        functools.partial(_attn_prep_body, blk=blk),
        grid=(bsz, n_heads, nb // per),
        in_specs=[pl.BlockSpec((None, rows, V_DIM), lambda b, h, s: (b, s, h)),
                  pl.BlockSpec((None, rows, V_DIM), lambda b, h, s: (b, s, n_heads + h)),
                  pl.BlockSpec((None, rows, V_DIM), lambda b, h, s: (b, s, 2 * n_heads + h)),
                  gain_spec, gain_spec,
                  pl.BlockSpec((None, 1, V_DIM), lambda b, h, s: (h, 0, 0))],
        out_specs=[pl.BlockSpec((None, None, 2, V_DIM, rows), lambda b, h, s: (b, h, 0, 0, s)),
                   pl.BlockSpec((None, None, 2, rows, V_DIM), lambda b, h, s: (b, h, 0, s, 0)),
                   pl.BlockSpec((None, None, per, VT_ROWS, blk), lambda b, h, s: (b, h, s, 0, 0))],
        out_shape=[jax.ShapeDtypeStruct((bsz, n_heads, 2, V_DIM, seq), BF16),
                   jax.ShapeDtypeStruct((bsz, n_heads, 2, seq, V_DIM), BF16),
                   jax.ShapeDtypeStruct((bsz, n_heads, nb, VT_ROWS, blk), BF16)],
        compiler_params=_params(("parallel", "parallel", "parallel"), 32),
        name="attn_prep",
    )(z, z, z, jnp.tile(q_gain, 2).reshape(1, V_DIM), jnp.tile(k_gain, 2).reshape(1, V_DIM), q_aug)


def _attn_body(scal_ref, sl_ref, qt_ref, k_ref, vt_ref, sg_ref, o_ref,
               m_s, acc_s, st_a, st_b, *, blk, hps):
    hg = pl.program_id(1)
    qi = pl.program_id(2)
    n_chains = 2 * hps
    m_s[...] = jnp.full_like(m_s, NEG_BIG)
    acc_s[...] = jnp.zeros_like(acc_s)
    st_bufs = (st_a, st_b)

    def scores(j, dst):
        r = pl.multiple_of(j * blk, blk)
        for c in range(n_chains):
            st_bufs[dst][c] = jnp.dot(k_ref[c // 2, c % 2, pl.ds(r, blk), :], qt_ref[c // 2, c % 2],
                                      preferred_element_type=F32)

    def absorb(j, src, diagonal):
        for c in range(n_chains):
            off = sl_ref[hg * hps + c // 2] * ((j - qi) * blk).astype(F32)
            st = st_bufs[src][c]
            if diagonal:
                key = lax.broadcasted_iota(jnp.int32, st.shape, 0)
                qry = lax.broadcasted_iota(jnp.int32, st.shape, 1)
                st = jnp.where(key <= qry, st, -jnp.inf)
            m_old = m_s[c]
            m_new = jnp.maximum(m_old, jnp.max(st, axis=0, keepdims=True) + off)
            alpha = jnp.exp2(m_old - m_new)
            pt = jnp.exp2(st - (m_new - off)).astype(BF16)
            acc_s[c] = alpha * acc_s[c] + jnp.dot(vt_ref[c // 2, j], pt,
                                                  preferred_element_type=F32)
            m_s[c] = m_new

    scores(0, 0)

    def pair(t, carry):
        j = 2 * t
        scores(j + 1, 1)
        absorb(j, 0, False)
        scores(j + 2, 0)
        absorb(j + 1, 1, False)
        return carry
    lax.fori_loop(0, qi // 2, pair, 0)

    @pl.when(qi % 2 == 0)
    def _():
        absorb(qi, 0, True)

    @pl.when(qi % 2 == 1)
    def _():
        scores(qi, 1)
        absorb(qi - 1, 0, False)
        absorb(qi, 1, True)

    lam = scal_ref[0]
    for hh in range(hps):
        a0 = acc_s[2 * hh]
        a1 = acc_s[2 * hh + 1]
        o = a0[:V_DIM] / a0[V_DIM:V_DIM + 1] - lam * (a1[:V_DIM] / a1[V_DIM:V_DIM + 1])
        ms = jnp.mean(o * o, axis=0, keepdims=True)
        y = ((o * lax.rsqrt(ms + EPS)) * sg_ref[...]) * scal_ref[1]
        o_ref[:, hh * V_DIM:(hh + 1) * V_DIM] = y.T.astype(o_ref.dtype)


def _bf16_part(x):
    bits = lax.bitcast_convert_type(x, jnp.uint32) & jnp.uint32(0xFFFF0000)
    return lax.bitcast_convert_type(bits, F32)


def _alibi_columns(n_heads):
    slopes = jnp.exp2(-8.0 * jnp.arange(1, n_heads + 1, dtype=F32) / n_heads)
    sl = slopes * LOG2E
    a1 = _bf16_part(sl)
    a2 = _bf16_part(sl - a1)
    a3 = _bf16_part(sl - a1 - a2)
    pieces = jnp.stack([a1, a2, a3, a1, a2, a3], axis=1)
    q_aug = jnp.zeros((n_heads, 1, V_DIM), F32).at[:, 0, QK_DIM:QK_DIM + ALIBI_COLS].set(pieces)
    return a1 + a2 + a3, q_aug


def _diff_attention(z, scal, q_gain, k_gain, subln_gain, n_heads):
    bsz, seq, _ = z.shape
    blk = _tile(seq, ATTN_BLOCK)
    sl, q_aug = _alibi_columns(n_heads)
    qt, kp, vt = _attn_prep(z, q_gain, k_gain, q_aug, n_heads, blk)
    hps = ATTN_HEADS_PER_STEP if n_heads % ATTN_HEADS_PER_STEP == 0 else 1
    n_chains = 2 * hps
    smem = pl.BlockSpec(memory_space=pltpu.SMEM)
    return pl.pallas_call(
        functools.partial(_attn_body, blk=blk, hps=hps),
        grid=(bsz, n_heads // hps, seq // blk),
        in_specs=[smem, smem,
                  pl.BlockSpec((None, hps, 2, V_DIM, blk), lambda b, h, i: (b, h, 0, 0, i)),
                  pl.BlockSpec((None, hps, 2, seq, V_DIM), lambda b, h, i: (b, h, 0, 0, 0),
                               pipeline_mode=pl.Buffered(1)),
                  pl.BlockSpec((None, hps, seq // blk, VT_ROWS, blk), lambda b, h, i: (b, h, 0, 0, 0),
                               pipeline_mode=pl.Buffered(1)),
                  pl.BlockSpec((V_DIM, 1), lambda b, h, i: (0, 0))],
        out_specs=pl.BlockSpec((None, blk, hps * V_DIM), lambda b, h, i: (b, i, h)),
        out_shape=jax.ShapeDtypeStruct((bsz, seq, n_heads * V_DIM), BF16),
        scratch_shapes=[pltpu.VMEM((n_chains, 1, blk), F32),
                        pltpu.VMEM((n_chains, VT_ROWS, blk), F32),
                        pltpu.VMEM((n_chains, blk, blk), F32),
                        pltpu.VMEM((n_chains, blk, blk), F32)],
        compiler_params=_params(("parallel", "parallel", "arbitrary"), 56),
        name="diff_attn",
    )(scal, sl, qt, kp, vt, subln_gain.reshape(V_DIM, 1))


def _conv_body(b_ref, c_ref, u_ref, w_ref, g_ref, o_ref, tail_s):
    si = pl.program_id(2)
    rows, cols = o_ref.shape

    @pl.when(si == 0)
    def _():
        tail_s[...] = jnp.zeros_like(tail_s)

    cu = c_ref[...] * u_ref[...]
    row = lax.broadcasted_iota(jnp.int32, (rows, cols), 0)
    tail = tail_s[...]
    prev1 = jnp.where(row == 0, tail[7:8, :], pltpu.roll(cu, 1, axis=0))
    prev2 = jnp.where(row == 0, tail[6:7, :],
                      jnp.where(row == 1, tail[7:8, :], pltpu.roll(cu, 2, axis=0)))
    acc = prev2 * w_ref[0:1, :]
    acc = acc + prev1 * w_ref[1:2, :]
    acc = acc + cu * w_ref[2:3, :]
    y = b_ref[...] * acc
    tail_s[...] = cu[rows - 8:, :]
    for gi in range(cols // CONV_GROUP_DIM):
        sl = slice(gi * CONV_GROUP_DIM, (gi + 1) * CONV_GROUP_DIM)
        yg = y[:, sl]
        ms = jnp.mean(yg * yg, axis=-1, keepdims=True)
        o_ref[:, sl] = ((yg * lax.rsqrt(ms + EPS)) * g_ref[...]).astype(o_ref.dtype)


def _gated_conv(z, col0, width, conv_w, group_gain):
    bsz, seq, _ = z.shape
    cb = _tile(width, 512)
    bs = _tile(seq, 512)
    nb = width // cb
    base = col0 // cb
    zspec = lambda slab: pl.BlockSpec(
        (None, bs, cb), lambda b, c, s: (b, s, base + slab * nb + c))
    return pl.pallas_call(
        _conv_body,
        grid=(bsz, nb, seq // bs),
        in_specs=[zspec(0), zspec(1), zspec(2),
                  pl.BlockSpec((CONV_K, cb), lambda b, c, s: (0, c)),
                  pl.BlockSpec((1, CONV_GROUP_DIM), lambda b, c, s: (0, 0))],
        out_specs=pl.BlockSpec((None, bs, cb), lambda b, c, s: (b, s, c)),
        out_shape=jax.ShapeDtypeStruct((bsz, seq, width), BF16),
        scratch_shapes=[pltpu.VMEM((8, cb), F32)],
        compiler_params=_params(("parallel", "parallel", "arbitrary"), 32),
        name="gated_conv",
    )(z, z, z, conv_w, group_gain.reshape(1, CONV_GROUP_DIM))


def _out_proj_body(a_ref, y_ref, wa_ref, wc_ref, x_ref, o_ref):
    mix = (jnp.dot(a_ref[...], wa_ref[...].astype(BF16), preferred_element_type=F32)
           + jnp.dot(y_ref[...], wc_ref[...].astype(BF16), preferred_element_type=F32))
    o_ref[...] = x_ref[...] + mix


def _out_proj(attn, conv, w_stack, layer, x):
    m, d = x.shape
    bm = _tile(m, 1024)
    bn = _tile(d, 512)
    ka, kc = attn.shape[1], conv.shape[1]
    assert ka == kc, "row blocks of the output projection are indexed in units of one head group"
    return pl.pallas_call(
        _out_proj_body,
        grid=(m // bm, d // bn),
        in_specs=[pl.BlockSpec((bm, ka), lambda i, j: (i, 0)),
                  pl.BlockSpec((bm, kc), lambda i, j: (i, 0)),
                  pl.BlockSpec((None, ka, bn), lambda i, j: (layer, 0, j)),
                  pl.BlockSpec((None, kc, bn), lambda i, j: (layer, 1, j)),
                  pl.BlockSpec((bm, bn), lambda i, j: (i, j))],
        out_specs=pl.BlockSpec((bm, bn), lambda i, j: (i, j)),
        out_shape=jax.ShapeDtypeStruct((m, d), F32),
        compiler_params=_params(("parallel", "arbitrary"), 52),
        name="out_proj",
    )(attn, conv, w_stack, w_stack, x)


def _ple_body(hg_ref, wg_ref, p_ref, wp_ref, pg_ref, x_ref, o_ref, e_s, inv_s):
    j = pl.program_id(1)
    n_tiles, _, bn = e_s.shape

    @pl.when(j == 0)
    def _():
        pb = p_ref[...].astype(BF16)
        ssq = jnp.zeros(inv_s.shape, F32)
        for t in range(n_tiles):
            e = jnp.dot(pb, wp_ref[:, t * bn:(t + 1) * bn].astype(BF16),
                        preferred_element_type=F32)
            e_s[t] = e
            ssq = ssq + jnp.sum(e * e, axis=-1, keepdims=True)
        inv_s[...] = lax.rsqrt(ssq / (n_tiles * bn) + EPS)

    gate = jax.nn.sigmoid(jnp.dot(hg_ref[...], wg_ref[...], preferred_element_type=F32))
    e = (e_s[j] * inv_s[...]) * pg_ref[...]
    o_ref[...] = x_ref[...] + gate * e


def _ple(hg, w_gate, p, w_proj_stack, layer, post_gain, x):
    m, d = x.shape
    pd = p.shape[1]
    bm = _tile(m, 512)
    bn = _tile(d, 1024)
    return pl.pallas_call(
        _ple_body,
        grid=(m // bm, d // bn),
        in_specs=[pl.BlockSpec((bm, d), lambda i, j: (i, 0)),
                  pl.BlockSpec((d, bn), lambda i, j: (0, j)),
                  pl.BlockSpec((bm, pd), lambda i, j: (i, 0)),
                  pl.BlockSpec((None, pd, d), lambda i, j: (layer, 0, 0)),
                  pl.BlockSpec((1, bn), lambda i, j: (0, j)),
                  pl.BlockSpec((bm, bn), lambda i, j: (i, j))],
        out_specs=pl.BlockSpec((bm, bn), lambda i, j: (i, j)),
        out_shape=jax.ShapeDtypeStruct((m, d), F32),
        scratch_shapes=[pltpu.VMEM((d // bn, bm, bn), F32),
                        pltpu.VMEM((bm, 1), F32)],
        compiler_params=_params(("parallel", "arbitrary"), 56),
        name="ple",
    )(hg, w_gate, p, w_proj_stack, post_gain.reshape(1, d), x)


def kernel(x, p, ffn1_norm, ffn1_w_gate, ffn1_w_up, ffn1_w_down, mix_norm, w_in, q_norm, k_norm, lambda_q1, lambda_k1, lambda_q2, lambda_k2, attn_subln, conv_w, conv_norm, w_out, ffn2_norm, ffn2_w_gate, ffn2_w_up, ffn2_w_down, ple_w_proj, ple_post_norm, ple_gate_norm, ple_w_gate):
    bsz, seq, d = x.shape
    depth = p.shape[0]
    m = bsz * seq
    attn_width = w_out.shape[1] - conv_w.shape[2]
    conv_width = conv_w.shape[2]
    n_heads = attn_width // V_DIM
    assert w_in.shape[2] == 3 * attn_width + 3 * conv_width

    xf = x.reshape(m, d)
    for i in range(depth):
        lam_init = 0.8 - 0.6 * math.exp(-0.3 * i)
        lam = (jnp.exp(jnp.sum(lambda_q1[i] * lambda_k1[i]))
               - jnp.exp(jnp.sum(lambda_q2[i] * lambda_k2[i])) + lam_init)
        scal = jnp.stack([lam, jnp.asarray(1.0 - lam_init, F32)]).astype(F32)

        xf = _ffn(xf, ffn1_norm[i], ffn1_w_gate, ffn1_w_up, ffn1_w_down, i)

        h = _rmsnorm(xf, mix_norm[i])
        z = _in_proj(h, w_in, i).reshape(bsz, seq, -1)
        attn = _diff_attention(z, scal, q_norm[i], k_norm[i], attn_subln[i], n_heads)
        conv = _gated_conv(z, 3 * attn_width, conv_width, conv_w[i], conv_norm[i])
        xf = _out_proj(attn.reshape(m, attn_width), conv.reshape(m, conv_width), w_out, i, xf)

        xf = _ffn(xf, ffn2_norm[i], ffn2_w_gate, ffn2_w_up, ffn2_w_down, i)

        hg = _rmsnorm(xf, ple_gate_norm[i])
        xf = _ple(hg, _cast_layer(ple_w_gate, i), p[i].reshape(m, -1), ple_w_proj, i,
                  ple_post_norm[i], xf)
    return xf.reshape(bsz, seq, d)
```

```python
import functools
import math

import jax
import jax.numpy as jnp
from jax import lax
from jax.experimental import pallas as pl
from jax.experimental.pallas import tpu as pltpu

F32 = jnp.float32
BF16 = jnp.bfloat16

EPS = 1e-6
HALF_STEP = 0.5
QK_DIM = 64
V_DIM = 2 * QK_DIM
CONV_GROUP_DIM = 128
CONV_K = 3
LANES = 128
PLE_DIM = 256
NEG_BIG = -1e30
LOG2E = 1.4426950408889634
ALIBI_COLS = 6
ONES_ROWS = 16
VT_ROWS = V_DIM + ONES_ROWS
ATTN_BLOCK = 512
ATTN_HEADS_PER_STEP = 4
PREP_BLOCKS = 2

MIB = 1024 * 1024


def _tile(dim, pref):
    t = min(dim, pref)
    while dim % t:
        t //= 2
    return t


def _params(semantics, vmem_mib):
    return pltpu.CompilerParams(dimension_semantics=semantics,
                                vmem_limit_bytes=vmem_mib * MIB)


def _emit_normed(o, gain_ref, xg_ref, ssq_ref, first):
    xg_ref[...] = (o * gain_ref[...]).astype(xg_ref.dtype)
    part = jnp.broadcast_to(jnp.sum(o * o, axis=-1, keepdims=True), ssq_ref.shape)

    @pl.when(first)
    def _():
        ssq_ref[...] = part

    @pl.when(jnp.logical_not(first))
    def _():
        ssq_ref[...] += part


def _inv_rms(ssq_ref, width):
    return lax.rsqrt(ssq_ref[:, 0:1] / width + EPS)


def _prescale_body(x_ref, g_ref, xg_ref, ssq_ref):
    x = x_ref[...]
    xg_ref[...] = (x * g_ref[...]).astype(xg_ref.dtype)
    ssq_ref[...] = jnp.broadcast_to(jnp.sum(x * x, axis=-1, keepdims=True), ssq_ref.shape)


def _prescale(x, gain):
    m, d = x.shape
    bm = _tile(m, 512)
    return pl.pallas_call(
        _prescale_body,
        grid=(m // bm,),
        in_specs=[pl.BlockSpec((bm, d), lambda i: (i, 0)),
                  pl.BlockSpec((1, d), lambda i: (0, 0))],
        out_specs=[pl.BlockSpec((bm, d), lambda i: (i, 0)),
                   pl.BlockSpec((bm, LANES), lambda i: (i, 0))],
        out_shape=[jax.ShapeDtypeStruct((m, d), BF16),
                   jax.ShapeDtypeStruct((m, LANES), F32)],
        compiler_params=_params(("parallel",), 40),
        name="prescale",
    )(x, gain.reshape(1, d))


def _cast_body(w_ref, o_ref):
    o_ref[...] = w_ref[...].astype(o_ref.dtype)


def _cast_layer(w_stack, layer):
    _, r, c = w_stack.shape
    br = _tile(r, 256)
    return pl.pallas_call(
        _cast_body,
        grid=(r // br,),
        in_specs=[pl.BlockSpec((None, br, c), lambda i: (layer, i, 0))],
        out_specs=pl.BlockSpec((br, c), lambda i: (i, 0)),
        out_shape=jax.ShapeDtypeStruct((r, c), BF16),
        compiler_params=_params(("parallel",), 40),
        name="cast_bf16",
    )(w_stack)


def _gate_up_body(xg_ref, ssq_ref, wg_ref, wu_ref, a_ref):
    xg = xg_ref[...]
    inv = _inv_rms(ssq_ref, xg.shape[1])
    g = jnp.dot(xg, wg_ref[...].astype(BF16), preferred_element_type=F32) * inv
    u = jnp.dot(xg, wu_ref[...].astype(BF16), preferred_element_type=F32) * inv
    a_ref[...] = ((g * jax.nn.sigmoid(g)) * u).astype(a_ref.dtype)


def _gate_up(xg, ssq, wg_stack, wu_stack, layer):
    m, d = xg.shape
    f = wg_stack.shape[2]
    bm = _tile(m, 1024)
    bf = _tile(f, 256)
    wspec = pl.BlockSpec((None, d, bf), lambda i, k: (layer, 0, k))
    return pl.pallas_call(
        _gate_up_body,
        grid=(m // bm, f // bf),
        in_specs=[pl.BlockSpec((bm, d), lambda i, k: (i, 0)),
                  pl.BlockSpec((bm, LANES), lambda i, k: (i, 0)), wspec, wspec],
        out_specs=pl.BlockSpec((bm, bf), lambda i, k: (i, k)),
        out_shape=jax.ShapeDtypeStruct((m, f), BF16),
        compiler_params=_params(("parallel", "arbitrary"), 48),
        name="ffn_gate_up",
    )(xg, ssq, wg_stack, wu_stack)


def _down_body(a_ref, wd_ref, x_ref, gain_ref, o_ref, xg_ref, ssq_ref):
    o = x_ref[...] + HALF_STEP * jnp.dot(a_ref[...], wd_ref[...], preferred_element_type=F32)
    o_ref[...] = o
    _emit_normed(o, gain_ref, xg_ref, ssq_ref, pl.program_id(1) == 0)


def _down(a, wd, x, next_gain):
    m, d = x.shape
    f = a.shape[1]
    bm = _tile(m, 512)
    bn = _tile(d, 512)
    tile = pl.BlockSpec((bm, bn), lambda i, j: (i, j))
    return pl.pallas_call(
        _down_body,
        grid=(m // bm, d // bn),
        in_specs=[pl.BlockSpec((bm, f), lambda i, j: (i, 0)),
                  pl.BlockSpec((f, bn), lambda i, j: (0, j)),
                  tile,
                  pl.BlockSpec((1, bn), lambda i, j: (0, j))],
        out_specs=[tile, tile, pl.BlockSpec((bm, LANES), lambda i, j: (i, 0))],
        out_shape=[jax.ShapeDtypeStruct((m, d), F32), jax.ShapeDtypeStruct((m, d), BF16),
                   jax.ShapeDtypeStruct((m, LANES), F32)],
        compiler_params=_params(("parallel", "arbitrary"), 56),
        name="ffn_down",
    )(a, wd, x, next_gain.reshape(1, d))


def _ffn(x, xg, ssq, wg_stack, wu_stack, wd_stack, layer, next_gain):
    a = _gate_up(xg, ssq, wg_stack, wu_stack, layer)
    return _down(a, _cast_layer(wd_stack, layer), x, next_gain)


def _in_proj_body(xg_ref, ssq_ref, w_ref, o_ref):
    xg = xg_ref[...]
    o_ref[...] = (jnp.dot(xg, w_ref[...].astype(BF16), preferred_element_type=F32)
                  * _inv_rms(ssq_ref, xg.shape[1]))


def _in_proj(xg, ssq, w_stack, layer):
    m, kd = xg.shape
    n = w_stack.shape[2]
    bm = _tile(m, 1024)
    bn = _tile(n, 512)
    return pl.pallas_call(
        _in_proj_body,
        grid=(m // bm, n // bn),
        in_specs=[pl.BlockSpec((bm, kd), lambda i, j: (i, 0)),
                  pl.BlockSpec((bm, LANES), lambda i, j: (i, 0)),
                  pl.BlockSpec((None, kd, bn), lambda i, j: (layer, 0, j))],
        out_specs=pl.BlockSpec((bm, bn), lambda i, j: (i, j)),
        out_shape=jax.ShapeDtypeStruct((m, n), F32),
        compiler_params=_params(("parallel", "arbitrary"), 52),
        name="in_proj",
    )(xg, ssq, w_stack)


def _attn_prep_body(q_ref, k_ref, v_ref, qg_ref, kg_ref, qa_ref, qt_ref, kp_ref, vt_ref, *, blk):
    rows = q_ref.shape[0]
    lane = lax.broadcasted_iota(jnp.int32, (1, V_DIM), 1)
    lo = lane < QK_DIM

    def group_norm(t, gain):
        t2 = t * t
        s_lo = jnp.sum(jnp.where(lo, t2, 0.0), axis=-1, keepdims=True)
        s_hi = jnp.sum(jnp.where(lo, 0.0, t2), axis=-1, keepdims=True)
        ms = jnp.where(lo, s_lo, s_hi) * (1.0 / QK_DIM)
        return (t * lax.rsqrt(ms + EPS)) * gain

    qn = group_norm(q_ref[...], qg_ref[...]) * (QK_DIM ** -0.5 * LOG2E)
    qa = qa_ref[...]
    qt_ref[0] = jnp.where(lo, qn, qa).T.astype(BF16)
    qt_ref[1] = jnp.where(lo, pltpu.roll(qn, QK_DIM, axis=1), qa).T.astype(BF16)

    kn = group_norm(k_ref[...], kg_ref[...])
    pos = lax.broadcasted_iota(jnp.int32, (rows, V_DIM), 0) & (blk - 1)
    pos_lo = (pos & 255).astype(F32)
    pos_hi = (pos - (pos & 255)).astype(F32)
    ka = jnp.where((lane >= QK_DIM) & (lane < QK_DIM + 3), pos_lo,
                   jnp.where((lane >= QK_DIM + 3) & (lane < QK_DIM + ALIBI_COLS), pos_hi, 0.0))
    kp_ref[0] = jnp.where(lo, kn, ka).astype(BF16)
    kp_ref[1] = jnp.where(lo, pltpu.roll(kn, QK_DIM, axis=1), ka).astype(BF16)

    for t in range(rows // blk):
        vt_ref[t, 0:V_DIM, :] = v_ref[t * blk:(t + 1) * blk, :].T.astype(BF16)
        vt_ref[t, V_DIM:, :] = jnp.ones((ONES_ROWS, blk), BF16)


def _attn_prep(z, q_gain, k_gain, q_aug, n_heads, blk):
    bsz, seq, _ = z.shape
    assert blk & (blk - 1) == 0
    nb = seq // blk
    per = PREP_BLOCKS if nb % PREP_BLOCKS == 0 else 1
    rows = per * blk
    gain_spec = pl.BlockSpec((1, V_DIM), lambda b, h, s: (0, 0))
    return pl.pallas_call(
        functools.partial(_attn_prep_body, blk=blk),
        grid=(bsz, n_heads, nb // per),
        in_specs=[pl.BlockSpec((None, rows, V_DIM), lambda b, h, s: (b, s, h)),
                  pl.BlockSpec((None, rows, V_DIM), lambda b, h, s: (b, s, n_heads + h)),
                  pl.BlockSpec((None, rows, V_DIM), lambda b, h, s: (b, s, 2 * n_heads + h)),
                  gain_spec, gain_spec,
                  pl.BlockSpec((None, 1, V_DIM), lambda b, h, s: (h, 0, 0))],
        out_specs=[pl.BlockSpec((None, None, 2, V_DIM, rows), lambda b, h, s: (b, h, 0, 0, s)),
                   pl.BlockSpec((None, None, 2, rows, V_DIM), lambda b, h, s: (b, h, 0, s, 0)),
                   pl.BlockSpec((None, None, per, VT_ROWS, blk), lambda b, h, s: (b, h, s, 0, 0))],
        out_shape=[jax.ShapeDtypeStruct((bsz, n_heads, 2, V_DIM, seq), BF16),
                   jax.ShapeDtypeStruct((bsz, n_heads, 2, seq, V_DIM), BF16),
                   jax.ShapeDtypeStruct((bsz, n_heads, nb, VT_ROWS, blk), BF16)],
        compiler_params=_params(("parallel", "parallel", "parallel"), 32),
        name="attn_prep",
    )(z, z, z, jnp.tile(q_gain, 2).reshape(1, V_DIM), jnp.tile(k_gain, 2).reshape(1, V_DIM), q_aug)


def _attn_body(scal_ref, sl_ref, qt_ref, k_ref, vt_ref, sg_ref, o_ref,
               m_s, acc_s, st_a, st_b, *, blk, hps):
    hg = pl.program_id(1)
    qi = pl.program_id(2)
    n_chains = 2 * hps
    m_s[...] = jnp.full_like(m_s, NEG_BIG)
    acc_s[...] = jnp.zeros_like(acc_s)
    st_bufs = (st_a, st_b)

    def scores(j, dst):
        r = pl.multiple_of(j * blk, blk)
        for c in range(n_chains):
            st_bufs[dst][c] = jnp.dot(k_ref[c // 2, c % 2, pl.ds(r, blk), :], qt_ref[c // 2, c % 2],
                                      preferred_element_type=F32)

    def absorb(j, src, diagonal):
        for c in range(n_chains):
            off = sl_ref[hg * hps + c // 2] * ((j - qi) * blk).astype(F32)
            st = st_bufs[src][c]
            if diagonal:
                key = lax.broadcasted_iota(jnp.int32, st.shape, 0)
                qry = lax.broadcasted_iota(jnp.int32, st.shape, 1)
                st = jnp.where(key <= qry, st, -jnp.inf)
            m_old = m_s[c]
            m_new = jnp.maximum(m_old, jnp.max(st, axis=0, keepdims=True) + off)
            alpha = jnp.exp2(m_old - m_new)
            pt = jnp.exp2(st - (m_new - off)).astype(BF16)
            acc_s[c] = alpha * acc_s[c] + jnp.dot(vt_ref[c // 2, j], pt,
                                                  preferred_element_type=F32)
            m_s[c] = m_new

    scores(0, 0)

    def pair(t, carry):
        j = 2 * t
        scores(j + 1, 1)
        absorb(j, 0, False)
        scores(j + 2, 0)
        absorb(j + 1, 1, False)
        return carry
    lax.fori_loop(0, qi // 2, pair, 0)

    @pl.when(qi % 2 == 0)
    def _():
        absorb(qi, 0, True)

    @pl.when(qi % 2 == 1)
    def _():
        scores(qi, 1)
        absorb(qi - 1, 0, False)
        absorb(qi, 1, True)

    lam = scal_ref[0]
    for hh in range(hps):
        a0 = acc_s[2 * hh]
        a1 = acc_s[2 * hh + 1]
        o = a0[:V_DIM] / a0[V_DIM:V_DIM + 1] - lam * (a1[:V_DIM] / a1[V_DIM:V_DIM + 1])
        ms = jnp.mean(o * o, axis=0, keepdims=True)
        y = ((o * lax.rsqrt(ms + EPS)) * sg_ref[...]) * scal_ref[1]
        o_ref[:, hh * V_DIM:(hh + 1) * V_DIM] = y.T.astype(o_ref.dtype)


def _bf16_part(x):
    bits = lax.bitcast_convert_type(x, jnp.uint32) & jnp.uint32(0xFFFF0000)
    return lax.bitcast_convert_type(bits, F32)


def _alibi_columns(n_heads):
    slopes = jnp.exp2(-8.0 * jnp.arange(1, n_heads + 1, dtype=F32) / n_heads)
    sl = slopes * LOG2E
    a1 = _bf16_part(sl)
    a2 = _bf16_part(sl - a1)
    a3 = _bf16_part(sl - a1 - a2)
    pieces = jnp.stack([a1, a2, a3, a1, a2, a3], axis=1)
    q_aug = jnp.zeros((n_heads, 1, V_DIM), F32).at[:, 0, QK_DIM:QK_DIM + ALIBI_COLS].set(pieces)
    return a1 + a2 + a3, q_aug


def _diff_attention(z, scal, q_gain, k_gain, subln_gain, n_heads):
    bsz, seq, _ = z.shape
    blk = _tile(seq, ATTN_BLOCK)
    sl, q_aug = _alibi_columns(n_heads)
    qt, kp, vt = _attn_prep(z, q_gain, k_gain, q_aug, n_heads, blk)
    hps = ATTN_HEADS_PER_STEP if n_heads % ATTN_HEADS_PER_STEP == 0 else 1
    n_chains = 2 * hps
    smem = pl.BlockSpec(memory_space=pltpu.SMEM)
    return pl.pallas_call(
        functools.partial(_attn_body, blk=blk, hps=hps),
        grid=(bsz, n_heads // hps, seq // blk),
        in_specs=[smem, smem,
                  pl.BlockSpec((None, hps, 2, V_DIM, blk), lambda b, h, i: (b, h, 0, 0, i)),
                  pl.BlockSpec((None, hps, 2, seq, V_DIM), lambda b, h, i: (b, h, 0, 0, 0),
                               pipeline_mode=pl.Buffered(1)),
                  pl.BlockSpec((None, hps, seq // blk, VT_ROWS, blk), lambda b, h, i: (b, h, 0, 0, 0),
                               pipeline_mode=pl.Buffered(1)),
                  pl.BlockSpec((V_DIM, 1), lambda b, h, i: (0, 0))],
        out_specs=pl.BlockSpec((None, blk, hps * V_DIM), lambda b, h, i: (b, i, h)),
        out_shape=jax.ShapeDtypeStruct((bsz, seq, n_heads * V_DIM), BF16),
        scratch_shapes=[pltpu.VMEM((n_chains, 1, blk), F32),
                        pltpu.VMEM((n_chains, VT_ROWS, blk), F32),
                        pltpu.VMEM((n_chains, blk, blk), F32),
                        pltpu.VMEM((n_chains, blk, blk), F32)],
        compiler_params=_params(("parallel", "parallel", "arbitrary"), 56),
        name="diff_attn",
    )(scal, sl, qt, kp, vt, subln_gain.reshape(V_DIM, 1))


def _conv_body(b_ref, c_ref, u_ref, w_ref, g_ref, o_ref, tail_s):
    si = pl.program_id(2)
    rows, cols = o_ref.shape

    @pl.when(si == 0)
    def _():
        tail_s[...] = jnp.zeros_like(tail_s)

    cu = c_ref[...] * u_ref[...]
    row = lax.broadcasted_iota(jnp.int32, (rows, cols), 0)
    tail = tail_s[...]
    prev1 = jnp.where(row == 0, tail[7:8, :], pltpu.roll(cu, 1, axis=0))
    prev2 = jnp.where(row == 0, tail[6:7, :],
                      jnp.where(row == 1, tail[7:8, :], pltpu.roll(cu, 2, axis=0)))
    acc = prev2 * w_ref[0:1, :]
    acc = acc + prev1 * w_ref[1:2, :]
    acc = acc + cu * w_ref[2:3, :]
    y = b_ref[...] * acc
    tail_s[...] = cu[rows - 8:, :]
    for gi in range(cols // CONV_GROUP_DIM):
        sl = slice(gi * CONV_GROUP_DIM, (gi + 1) * CONV_GROUP_DIM)
        yg = y[:, sl]
        ms = jnp.mean(yg * yg, axis=-1, keepdims=True)
        o_ref[:, sl] = ((yg * lax.rsqrt(ms + EPS)) * g_ref[...]).astype(o_ref.dtype)


def _gated_conv(z, col0, width, conv_w, group_gain):
    bsz, seq, _ = z.shape
    cb = _tile(width, 512)
    bs = _tile(seq, 512)
    nb = width // cb
    base = col0 // cb
    zspec = lambda slab: pl.BlockSpec(
        (None, bs, cb), lambda b, c, s: (b, s, base + slab * nb + c))
    return pl.pallas_call(
        _conv_body,
        grid=(bsz, nb, seq // bs),
        in_specs=[zspec(0), zspec(1), zspec(2),
                  pl.BlockSpec((CONV_K, cb), lambda b, c, s: (0, c)),
                  pl.BlockSpec((1, CONV_GROUP_DIM), lambda b, c, s: (0, 0))],
        out_specs=pl.BlockSpec((None, bs, cb), lambda b, c, s: (b, s, c)),
        out_shape=jax.ShapeDtypeStruct((bsz, seq, width), BF16),
        scratch_shapes=[pltpu.VMEM((8, cb), F32)],
        compiler_params=_params(("parallel", "parallel", "arbitrary"), 32),
        name="gated_conv",
    )(z, z, z, conv_w, group_gain.reshape(1, CONV_GROUP_DIM))


def _out_proj_body(a_ref, y_ref, wa_ref, wc_ref, x_ref, gain_ref, o_ref, xg_ref, ssq_ref):
    mix = (jnp.dot(a_ref[...], wa_ref[...].astype(BF16), preferred_element_type=F32)
           + jnp.dot(y_ref[...], wc_ref[...].astype(BF16), preferred_element_type=F32))
    o = x_ref[...] + mix
    o_ref[...] = o
    _emit_normed(o, gain_ref, xg_ref, ssq_ref, pl.program_id(1) == 0)


def _out_proj(attn, conv, w_stack, layer, x, next_gain):
    m, d = x.shape
    bm = _tile(m, 1024)
    bn = _tile(d, 512)
    ka, kc = attn.shape[1], conv.shape[1]
    assert ka == kc, "row blocks of the output projection are indexed in units of one head group"
    tile = pl.BlockSpec((bm, bn), lambda i, j: (i, j))
    return pl.pallas_call(
        _out_proj_body,
        grid=(m // bm, d // bn),
        in_specs=[pl.BlockSpec((bm, ka), lambda i, j: (i, 0)),
                  pl.BlockSpec((bm, kc), lambda i, j: (i, 0)),
                  pl.BlockSpec((None, ka, bn), lambda i, j: (layer, 0, j)),
                  pl.BlockSpec((None, kc, bn), lambda i, j: (layer, 1, j)),
                  tile,
                  pl.BlockSpec((1, bn), lambda i, j: (0, j))],
        out_specs=[tile, tile, pl.BlockSpec((bm, LANES), lambda i, j: (i, 0))],
        out_shape=[jax.ShapeDtypeStruct((m, d), F32), jax.ShapeDtypeStruct((m, d), BF16),
                   jax.ShapeDtypeStruct((m, LANES), F32)],
        compiler_params=_params(("parallel", "arbitrary"), 52),
        name="out_proj",
    )(attn, conv, w_stack, w_stack, x, next_gain.reshape(1, d))


def _ple_body(xg_ref, ssq_ref, wg_ref, p_ref, wp_ref, pg_ref, x_ref, *rest, emit_next):
    if emit_next:
        gain_ref, o_ref, nxg_ref, nssq_ref, e_s, inv_s = rest
    else:
        o_ref, e_s, inv_s = rest
    j = pl.program_id(1)
    n_tiles, _, bn = e_s.shape

    @pl.when(j == 0)
    def _():
        pb = p_ref[...].astype(BF16)
        ssq = jnp.zeros(inv_s.shape, F32)
        for t in range(n_tiles):
            e = jnp.dot(pb, wp_ref[:, t * bn:(t + 1) * bn].astype(BF16),
                        preferred_element_type=F32)
            e_s[t] = e
            ssq = ssq + jnp.sum(e * e, axis=-1, keepdims=True)
        inv_s[...] = lax.rsqrt(ssq / (n_tiles * bn) + EPS)

    xg = xg_ref[...]
    gate = jax.nn.sigmoid(jnp.dot(xg, wg_ref[...], preferred_element_type=F32)
                          * _inv_rms(ssq_ref, xg.shape[1]))
    e = (e_s[j] * inv_s[...]) * pg_ref[...]
    o = x_ref[...] + gate * e
    o_ref[...] = o
    if emit_next:
        _emit_normed(o, gain_ref, nxg_ref, nssq_ref, j == 0)


def _ple(xg, ssq, w_gate, p, w_proj_stack, layer, post_gain, x, next_gain):
    m, d = x.shape
    pd = p.shape[1]
    bm = _tile(m, 512)
    bn = _tile(d, 1024)
    emit_next = next_gain is not None
    tile = pl.BlockSpec((bm, bn), lambda i, j: (i, j))
    row = pl.BlockSpec((1, bn), lambda i, j: (0, j))
    stats = pl.BlockSpec((bm, LANES), lambda i, j: (i, 0))
    in_specs = [pl.BlockSpec((bm, d), lambda i, j: (i, 0)), stats,
                pl.BlockSpec((d, bn), lambda i, j: (0, j)),
                pl.BlockSpec((bm, pd), lambda i, j: (i, 0)),
                pl.BlockSpec((None, pd, d), lambda i, j: (layer, 0, 0)),
                row, tile]
    args = [xg, ssq, w_gate, p, w_proj_stack, post_gain.reshape(1, d), x]
    out_specs = [tile]
    out_shape = [jax.ShapeDtypeStruct((m, d), F32)]
    if emit_next:
        in_specs.append(row)
        args.append(next_gain.reshape(1, d))
        out_specs += [tile, stats]
        out_shape += [jax.ShapeDtypeStruct((m, d), BF16), jax.ShapeDtypeStruct((m, LANES), F32)]
    return pl.pallas_call(
        functools.partial(_ple_body, emit_next=emit_next),
        grid=(m // bm, d // bn),
        in_specs=in_specs,
        out_specs=out_specs,
        out_shape=out_shape,
        scratch_shapes=[pltpu.VMEM((d // bn, bm, bn), F32),
                        pltpu.VMEM((bm, 1), F32)],
        compiler_params=_params(("parallel", "arbitrary"), 56),
        name="ple",
    )(*args)


def kernel(x, p, ffn1_norm, ffn1_w_gate, ffn1_w_up, ffn1_w_down, mix_norm, w_in, q_norm, k_norm, lambda_q1, lambda_k1, lambda_q2, lambda_k2, attn_subln, conv_w, conv_norm, w_out, ffn2_norm, ffn2_w_gate, ffn2_w_up, ffn2_w_down, ple_w_proj, ple_post_norm, ple_gate_norm, ple_w_gate):
    bsz, seq, d = x.shape
    depth = p.shape[0]
    m = bsz * seq
    attn_width = w_out.shape[1] - conv_w.shape[2]
    conv_width = conv_w.shape[2]
    n_heads = attn_width // V_DIM
    assert w_in.shape[2] == 3 * attn_width + 3 * conv_width

    xf = x.reshape(m, d)
    xg, ssq = _prescale(xf, ffn1_norm[0])
    for i in range(depth):
        lam_init = 0.8 - 0.6 * math.exp(-0.3 * i)
        lam = (jnp.exp(jnp.sum(lambda_q1[i] * lambda_k1[i]))
               - jnp.exp(jnp.sum(lambda_q2[i] * lambda_k2[i])) + lam_init)
        scal = jnp.stack([lam, jnp.asarray(1.0 - lam_init, F32)]).astype(F32)

        xf, xg, ssq = _ffn(xf, xg, ssq, ffn1_w_gate, ffn1_w_up, ffn1_w_down, i, mix_norm[i])

        z = _in_proj(xg, ssq, w_in, i).reshape(bsz, seq, -1)
        attn = _diff_attention(z, scal, q_norm[i], k_norm[i], attn_subln[i], n_heads)
        conv = _gated_conv(z, 3 * attn_width, conv_width, conv_w[i], conv_norm[i])
        xf, xg, ssq = _out_proj(attn.reshape(m, attn_width), conv.reshape(m, conv_width),
                                w_out, i, xf, ffn2_norm[i])

        xf, xg, ssq = _ffn(xf, xg, ssq, ffn2_w_gate, ffn2_w_up, ffn2_w_down, i,
                           ple_gate_norm[i])

        next_gain = ffn1_norm[i + 1] if i + 1 < depth else None
        res = _ple(xg, ssq, _cast_layer(ple_w_gate, i), p[i].reshape(m, -1), ple_w_proj, i,
                   ple_post_norm[i], xf, next_gain)
        xf, xg, ssq = res if next_gain is not None else (res[0], None, None)
    return xf.reshape(bsz, seq, d)
```

```python
import functools
import math

import jax
import jax.numpy as jnp
from jax import lax
from jax.experimental import pallas as pl
from jax.experimental.pallas import tpu as pltpu

F32 = jnp.float32
BF16 = jnp.bfloat16

EPS = 1e-6
HALF_STEP = 0.5
QK_DIM = 64
V_DIM = 2 * QK_DIM
CONV_GROUP_DIM = 128
CONV_K = 3
LANES = 128
PLE_DIM = 256
NEG_BIG = -1e30
LOG2E = 1.4426950408889634
ALIBI_COLS = 6
ONES_ROWS = 16
VT_ROWS = V_DIM + ONES_ROWS
ATTN_BLOCK = 512
ATTN_HEADS_PER_STEP = 4
PREP_BLOCKS = 2

MIB = 1024 * 1024


def _tile(dim, pref):
    t = min(dim, pref)
    while dim % t:
        t //= 2
    return t


def _params(semantics, vmem_mib):
    return pltpu.CompilerParams(dimension_semantics=semantics,
                                vmem_limit_bytes=vmem_mib * MIB)


def _emit_normed(o, gain_ref, xg_ref, ssq_ref, first):
    xg_ref[...] = (o * gain_ref[...]).astype(xg_ref.dtype)
    part = jnp.broadcast_to(jnp.sum(o * o, axis=-1, keepdims=True), ssq_ref.shape)

    @pl.when(first)
    def _():
        ssq_ref[...] = part

    @pl.when(jnp.logical_not(first))
    def _():
        ssq_ref[...] += part


def _inv_rms(ssq_ref, width):
    return lax.rsqrt(ssq_ref[:, 0:1] / width + EPS)


def _prescale_body(x_ref, g_ref, xg_ref, ssq_ref):
    x = x_ref[...]
    xg_ref[...] = (x * g_ref[...]).astype(xg_ref.dtype)
    ssq_ref[...] = jnp.broadcast_to(jnp.sum(x * x, axis=-1, keepdims=True), ssq_ref.shape)


def _prescale(x, gain):
    m, d = x.shape
    bm = _tile(m, 512)
    return pl.pallas_call(
        _prescale_body,
        grid=(m // bm,),
        in_specs=[pl.BlockSpec((bm, d), lambda i: (i, 0)),
                  pl.BlockSpec((1, d), lambda i: (0, 0))],
        out_specs=[pl.BlockSpec((bm, d), lambda i: (i, 0)),
                   pl.BlockSpec((bm, LANES), lambda i: (i, 0))],
        out_shape=[jax.ShapeDtypeStruct((m, d), BF16),
                   jax.ShapeDtypeStruct((m, LANES), F32)],
        compiler_params=_params(("parallel",), 40),
        name="prescale",
    )(x, gain.reshape(1, d))


def _cast_body(w_ref, o_ref):
    o_ref[...] = w_ref[...].astype(o_ref.dtype)


def _cast_layer(w_stack, layer):
    _, r, c = w_stack.shape
    br = _tile(r, 256)
    return pl.pallas_call(
        _cast_body,
        grid=(r // br,),
        in_specs=[pl.BlockSpec((None, br, c), lambda i: (layer, i, 0))],
        out_specs=pl.BlockSpec((br, c), lambda i: (i, 0)),
        out_shape=jax.ShapeDtypeStruct((r, c), BF16),
        compiler_params=_params(("parallel",), 40),
        name="cast_bf16",
    )(w_stack)


def _gate_up_body(xg_ref, ssq_ref, wg_ref, wu_ref, a_ref):
    xg = xg_ref[...]
    inv = _inv_rms(ssq_ref, xg.shape[1])
    g = jnp.dot(xg, wg_ref[...].astype(BF16), preferred_element_type=F32) * inv
    u = jnp.dot(xg, wu_ref[...].astype(BF16), preferred_element_type=F32) * inv
    a_ref[...] = ((g * jax.nn.sigmoid(g)) * u).astype(a_ref.dtype)


def _gate_up(xg, ssq, wg_stack, wu_stack, layer):
    m, d = xg.shape
    f = wg_stack.shape[2]
    bm = _tile(m, 1024)
    bf = _tile(f, 256)
    wspec = pl.BlockSpec((None, d, bf), lambda i, k: (layer, 0, k))
    return pl.pallas_call(
        _gate_up_body,
        grid=(m // bm, f // bf),
        in_specs=[pl.BlockSpec((bm, d), lambda i, k: (i, 0)),
                  pl.BlockSpec((bm, LANES), lambda i, k: (i, 0)), wspec, wspec],
        out_specs=pl.BlockSpec((bm, bf), lambda i, k: (i, k)),
        out_shape=jax.ShapeDtypeStruct((m, f), BF16),
        compiler_params=_params(("parallel", "arbitrary"), 48),
        name="ffn_gate_up",
    )(xg, ssq, wg_stack, wu_stack)


def _down_body(a_ref, wd_ref, x_ref, gain_ref, o_ref, xg_ref, ssq_ref):
    o = x_ref[...] + HALF_STEP * jnp.dot(a_ref[...], wd_ref[...], preferred_element_type=F32)
    o_ref[...] = o
    _emit_normed(o, gain_ref, xg_ref, ssq_ref, pl.program_id(1) == 0)


def _down(a, wd, x, next_gain):
    m, d = x.shape
    f = a.shape[1]
    bm = _tile(m, 512)
    bn = _tile(d, 512)
    tile = pl.BlockSpec((bm, bn), lambda i, j: (i, j))
    return pl.pallas_call(
        _down_body,
        grid=(m // bm, d // bn),
        in_specs=[pl.BlockSpec((bm, f), lambda i, j: (i, 0)),
                  pl.BlockSpec((f, bn), lambda i, j: (0, j)),
                  tile,
                  pl.BlockSpec((1, bn), lambda i, j: (0, j))],
        out_specs=[tile, tile, pl.BlockSpec((bm, LANES), lambda i, j: (i, 0))],
        out_shape=[jax.ShapeDtypeStruct((m, d), F32), jax.ShapeDtypeStruct((m, d), BF16),
                   jax.ShapeDtypeStruct((m, LANES), F32)],
        compiler_params=_params(("parallel", "arbitrary"), 56),
        name="ffn_down",
    )(a, wd, x, next_gain.reshape(1, d))


def _ffn(x, xg, ssq, wg_stack, wu_stack, wd_stack, layer, next_gain):
    a = _gate_up(xg, ssq, wg_stack, wu_stack, layer)
    return _down(a, _cast_layer(wd_stack, layer), x, next_gain)


def _in_proj_body(xg_ref, ssq_ref, w_ref, o_ref):
    xg = xg_ref[...]
    o_ref[...] = (jnp.dot(xg, w_ref[...].astype(BF16), preferred_element_type=F32)
                  * _inv_rms(ssq_ref, xg.shape[1]))


def _in_proj(xg, ssq, w_stack, layer):
    m, kd = xg.shape
    n = w_stack.shape[2]
    bm = _tile(m, 1024)
    bn = _tile(n, 512)
    return pl.pallas_call(
        _in_proj_body,
        grid=(m // bm, n // bn),
        in_specs=[pl.BlockSpec((bm, kd), lambda i, j: (i, 0)),
                  pl.BlockSpec((bm, LANES), lambda i, j: (i, 0)),
                  pl.BlockSpec((None, kd, bn), lambda i, j: (layer, 0, j))],
        out_specs=pl.BlockSpec((bm, bn), lambda i, j: (i, j)),
        out_shape=jax.ShapeDtypeStruct((m, n), F32),
        compiler_params=_params(("parallel", "arbitrary"), 52),
        name="in_proj",
    )(xg, ssq, w_stack)


def _attn_prep_body(q_ref, k_ref, v_ref, qg_ref, kg_ref, qa_ref, qt_ref, kp_ref, vt_ref, *, blk):
    rows = q_ref.shape[0]
    lane = lax.broadcasted_iota(jnp.int32, (1, V_DIM), 1)
    lo = lane < QK_DIM

    def group_norm(t, gain):
        t2 = t * t
        s_lo = jnp.sum(jnp.where(lo, t2, 0.0), axis=-1, keepdims=True)
        s_hi = jnp.sum(jnp.where(lo, 0.0, t2), axis=-1, keepdims=True)
        ms = jnp.where(lo, s_lo, s_hi) * (1.0 / QK_DIM)
        return (t * lax.rsqrt(ms + EPS)) * gain

    qn = group_norm(q_ref[...], qg_ref[...]) * (QK_DIM ** -0.5 * LOG2E)
    qt_ref[0] = jnp.where(lo, qn, qa_ref[0:1, :]).T.astype(BF16)
    qt_ref[1] = jnp.where(lo, qa_ref[1:2, :], qn).T.astype(BF16)

    kn = group_norm(k_ref[...], kg_ref[...])
    pos = lax.broadcasted_iota(jnp.int32, (rows, V_DIM), 0) & (blk - 1)
    pos_lo = (pos & 255).astype(F32)
    pos_hi = (pos - (pos & 255)).astype(F32)
    col = jnp.where(lo, lane + QK_DIM, lane)
    ka = jnp.where((col >= QK_DIM) & (col < QK_DIM + 3), pos_lo,
                   jnp.where((col >= QK_DIM + 3) & (col < QK_DIM + ALIBI_COLS), pos_hi, 0.0))
    kp_ref[0] = jnp.where(lo, kn, ka).astype(BF16)
    kp_ref[1] = jnp.where(lo, ka, kn).astype(BF16)

    for t in range(rows // blk):
        vt_ref[t, 0:V_DIM, :] = v_ref[t * blk:(t + 1) * blk, :].T.astype(BF16)
        vt_ref[t, V_DIM:, :] = jnp.ones((ONES_ROWS, blk), BF16)


def _attn_prep(z, q_gain, k_gain, q_aug, n_heads, blk):
    bsz, seq, _ = z.shape
    assert blk & (blk - 1) == 0
    nb = seq // blk
    per = PREP_BLOCKS if nb % PREP_BLOCKS == 0 else 1
    rows = per * blk
    gain_spec = pl.BlockSpec((1, V_DIM), lambda b, h, s: (0, 0))
    return pl.pallas_call(
        functools.partial(_attn_prep_body, blk=blk),
        grid=(bsz, n_heads, nb // per),
        in_specs=[pl.BlockSpec((None, rows, V_DIM), lambda b, h, s: (b, s, h)),
                  pl.BlockSpec((None, rows, V_DIM), lambda b, h, s: (b, s, n_heads + h)),
                  pl.BlockSpec((None, rows, V_DIM), lambda b, h, s: (b, s, 2 * n_heads + h)),
                  gain_spec, gain_spec,
                  pl.BlockSpec((None, 2, V_DIM), lambda b, h, s: (h, 0, 0))],
        out_specs=[pl.BlockSpec((None, None, 2, V_DIM, rows), lambda b, h, s: (b, h, 0, 0, s)),
                   pl.BlockSpec((None, None, 2, rows, V_DIM), lambda b, h, s: (b, h, 0, s, 0)),
                   pl.BlockSpec((None, None, per, VT_ROWS, blk), lambda b, h, s: (b, h, s, 0, 0))],
        out_shape=[jax.ShapeDtypeStruct((bsz, n_heads, 2, V_DIM, seq), BF16),
                   jax.ShapeDtypeStruct((bsz, n_heads, 2, seq, V_DIM), BF16),
                   jax.ShapeDtypeStruct((bsz, n_heads, nb, VT_ROWS, blk), BF16)],
        compiler_params=_params(("parallel", "parallel", "parallel"), 32),
        name="attn_prep",
    )(z, z, z, jnp.tile(q_gain, 2).reshape(1, V_DIM), jnp.tile(k_gain, 2).reshape(1, V_DIM), q_aug)


def _attn_body(scal_ref, sl_ref, qt_ref, k_ref, vt_ref, sg_ref, o_ref,
               m_s, acc_s, st_a, st_b, *, blk, hps):
    hg = pl.program_id(1)
    qi = pl.program_id(2)
    n_chains = 2 * hps
    m_s[...] = jnp.full_like(m_s, NEG_BIG)
    acc_s[...] = jnp.zeros_like(acc_s)
    st_bufs = (st_a, st_b)

    def scores(j, dst):
        r = pl.multiple_of(j * blk, blk)
        for c in range(n_chains):
            st_bufs[dst][c] = jnp.dot(k_ref[c // 2, c % 2, pl.ds(r, blk), :], qt_ref[c // 2, c % 2],
                                      preferred_element_type=F32)

    def absorb(j, src, diagonal):
        for c in range(n_chains):
            off = sl_ref[hg * hps + c // 2] * ((j - qi) * blk).astype(F32)
            st = st_bufs[src][c]
            if diagonal:
                key = lax.broadcasted_iota(jnp.int32, st.shape, 0)
                qry = lax.broadcasted_iota(jnp.int32, st.shape, 1)
                st = jnp.where(key <= qry, st, -jnp.inf)
            m_old = m_s[c]
            m_new = jnp.maximum(m_old, jnp.max(st, axis=0, keepdims=True) + off)
            alpha = jnp.exp2(m_old - m_new)
            pt = jnp.exp2(st - (m_new - off)).astype(BF16)
            acc_s[c] = alpha * acc_s[c] + jnp.dot(vt_ref[c // 2, j], pt,
                                                  preferred_element_type=F32)
            m_s[c] = m_new

    scores(0, 0)

    def pair(t, carry):
        j = 2 * t
        scores(j + 1, 1)
        absorb(j, 0, False)
        scores(j + 2, 0)
        absorb(j + 1, 1, False)
        return carry
    lax.fori_loop(0, qi // 2, pair, 0)

    @pl.when(qi % 2 == 0)
    def _():
        absorb(qi, 0, True)

    @pl.when(qi % 2 == 1)
    def _():
        scores(qi, 1)
        absorb(qi - 1, 0, False)
        absorb(qi, 1, True)

    lam = scal_ref[0]
    for hh in range(hps):
        a0 = acc_s[2 * hh]
        a1 = acc_s[2 * hh + 1]
        o = a0[:V_DIM] / a0[V_DIM:V_DIM + 1] - lam * (a1[:V_DIM] / a1[V_DIM:V_DIM + 1])
        ms = jnp.mean(o * o, axis=0, keepdims=True)
        y = ((o * lax.rsqrt(ms + EPS)) * sg_ref[...]) * scal_ref[1]
        o_ref[:, hh * V_DIM:(hh + 1) * V_DIM] = y.T.astype(o_ref.dtype)


def _bf16_part(x):
    bits = lax.bitcast_convert_type(x, jnp.uint32) & jnp.uint32(0xFFFF0000)
    return lax.bitcast_convert_type(bits, F32)


def _alibi_columns(n_heads):
    slopes = jnp.exp2(-8.0 * jnp.arange(1, n_heads + 1, dtype=F32) / n_heads)
    sl = slopes * LOG2E
    a1 = _bf16_part(sl)
    a2 = _bf16_part(sl - a1)
    a3 = _bf16_part(sl - a1 - a2)
    pieces = jnp.stack([a1, a2, a3, a1, a2, a3], axis=1)
    q_aug = jnp.zeros((n_heads, 2, V_DIM), F32)
    q_aug = q_aug.at[:, 0, QK_DIM:QK_DIM + ALIBI_COLS].set(pieces)
    q_aug = q_aug.at[:, 1, 0:ALIBI_COLS].set(pieces)
    return a1 + a2 + a3, q_aug


def _diff_attention(z, scal, q_gain, k_gain, subln_gain, n_heads):
    bsz, seq, _ = z.shape
    blk = _tile(seq, ATTN_BLOCK)
    sl, q_aug = _alibi_columns(n_heads)
    qt, kp, vt = _attn_prep(z, q_gain, k_gain, q_aug, n_heads, blk)
    hps = ATTN_HEADS_PER_STEP if n_heads % ATTN_HEADS_PER_STEP == 0 else 1
    n_chains = 2 * hps
    smem = pl.BlockSpec(memory_space=pltpu.SMEM)
    return pl.pallas_call(
        functools.partial(_attn_body, blk=blk, hps=hps),
        grid=(bsz, n_heads // hps, seq // blk),
        in_specs=[smem, smem,
                  pl.BlockSpec((None, hps, 2, V_DIM, blk), lambda b, h, i: (b, h, 0, 0, i)),
                  pl.BlockSpec((None, hps, 2, seq, V_DIM), lambda b, h, i: (b, h, 0, 0, 0),
                               pipeline_mode=pl.Buffered(1)),
                  pl.BlockSpec((None, hps, seq // blk, VT_ROWS, blk), lambda b, h, i: (b, h, 0, 0, 0),
                               pipeline_mode=pl.Buffered(1)),
                  pl.BlockSpec((V_DIM, 1), lambda b, h, i: (0, 0))],
        out_specs=pl.BlockSpec((None, blk, hps * V_DIM), lambda b, h, i: (b, i, h)),
        out_shape=jax.ShapeDtypeStruct((bsz, seq, n_heads * V_DIM), BF16),
        scratch_shapes=[pltpu.VMEM((n_chains, 1, blk), F32),
                        pltpu.VMEM((n_chains, VT_ROWS, blk), F32),
                        pltpu.VMEM((n_chains, blk, blk), F32),
                        pltpu.VMEM((n_chains, blk, blk), F32)],
        compiler_params=_params(("parallel", "parallel", "arbitrary"), 56),
        name="diff_attn",
    )(scal, sl, qt, kp, vt, subln_gain.reshape(V_DIM, 1))


def _conv_body(b_ref, c_ref, u_ref, w_ref, g_ref, o_ref, tail_s):
    si = pl.program_id(2)
    rows, cols = o_ref.shape

    @pl.when(si == 0)
    def _():
        tail_s[...] = jnp.zeros_like(tail_s)

    cu = c_ref[...] * u_ref[...]
    row = lax.broadcasted_iota(jnp.int32, (rows, cols), 0)
    tail = tail_s[...]
    prev1 = jnp.where(row == 0, tail[7:8, :], pltpu.roll(cu, 1, axis=0))
    prev2 = jnp.where(row == 0, tail[6:7, :],
                      jnp.where(row == 1, tail[7:8, :], pltpu.roll(cu, 2, axis=0)))
    acc = prev2 * w_ref[0:1, :]
    acc = acc + prev1 * w_ref[1:2, :]
    acc = acc + cu * w_ref[2:3, :]
    y = b_ref[...] * acc
    tail_s[...] = cu[rows - 8:, :]
    for gi in range(cols // CONV_GROUP_DIM):
        sl = slice(gi * CONV_GROUP_DIM, (gi + 1) * CONV_GROUP_DIM)
        yg = y[:, sl]
        ms = jnp.mean(yg * yg, axis=-1, keepdims=True)
        o_ref[:, sl] = ((yg * lax.rsqrt(ms + EPS)) * g_ref[...]).astype(o_ref.dtype)


def _gated_conv(z, col0, width, conv_w, group_gain):
    bsz, seq, _ = z.shape
    cb = _tile(width, 512)
    bs = _tile(seq, 1024)
    nb = width // cb
    base = col0 // cb
    zspec = lambda slab: pl.BlockSpec(
        (None, bs, cb), lambda b, c, s: (b, s, base + slab * nb + c))
    return pl.pallas_call(
        _conv_body,
        grid=(bsz, nb, seq // bs),
        in_specs=[zspec(0), zspec(1), zspec(2),
                  pl.BlockSpec((CONV_K, cb), lambda b, c, s: (0, c)),
                  pl.BlockSpec((1, CONV_GROUP_DIM), lambda b, c, s: (0, 0))],
        out_specs=pl.BlockSpec((None, bs, cb), lambda b, c, s: (b, s, c)),
        out_shape=jax.ShapeDtypeStruct((bsz, seq, width), BF16),
        scratch_shapes=[pltpu.VMEM((8, cb), F32)],
        compiler_params=_params(("parallel", "parallel", "arbitrary"), 32),
        name="gated_conv",
    )(z, z, z, conv_w, group_gain.reshape(1, CONV_GROUP_DIM))


def _out_proj_body(a_ref, y_ref, wa_ref, wc_ref, x_ref, gain_ref, o_ref, xg_ref, ssq_ref):
    mix = (jnp.dot(a_ref[...], wa_ref[...].astype(BF16), preferred_element_type=F32)
           + jnp.dot(y_ref[...], wc_ref[...].astype(BF16), preferred_element_type=F32))
    o = x_ref[...] + mix
    o_ref[...] = o
    _emit_normed(o, gain_ref, xg_ref, ssq_ref, pl.program_id(1) == 0)


def _out_proj(attn, conv, w_stack, layer, x, next_gain):
    m, d = x.shape
    bm = _tile(m, 1024)
    bn = _tile(d, 512)
    ka, kc = attn.shape[1], conv.shape[1]
    assert ka == kc, "row blocks of the output projection are indexed in units of one head group"
    tile = pl.BlockSpec((bm, bn), lambda i, j: (i, j))
    return pl.pallas_call(
        _out_proj_body,
        grid=(m // bm, d // bn),
        in_specs=[pl.BlockSpec((bm, ka), lambda i, j: (i, 0)),
                  pl.BlockSpec((bm, kc), lambda i, j: (i, 0)),
                  pl.BlockSpec((None, ka, bn), lambda i, j: (layer, 0, j)),
                  pl.BlockSpec((None, kc, bn), lambda i, j: (layer, 1, j)),
                  tile,
                  pl.BlockSpec((1, bn), lambda i, j: (0, j))],
        out_specs=[tile, tile, pl.BlockSpec((bm, LANES), lambda i, j: (i, 0))],
        out_shape=[jax.ShapeDtypeStruct((m, d), F32), jax.ShapeDtypeStruct((m, d), BF16),
                   jax.ShapeDtypeStruct((m, LANES), F32)],
        compiler_params=_params(("parallel", "arbitrary"), 52),
        name="out_proj",
    )(attn, conv, w_stack, w_stack, x, next_gain.reshape(1, d))


def _ple_body(xg_ref, ssq_ref, wg_ref, p_ref, wp_ref, pg_ref, x_ref, *rest, emit_next):
    if emit_next:
        gain_ref, o_ref, nxg_ref, nssq_ref, e_s, inv_s = rest
    else:
        o_ref, e_s, inv_s = rest
    j = pl.program_id(1)
    n_tiles, _, bn = e_s.shape

    @pl.when(j == 0)
    def _():
        pb = p_ref[...].astype(BF16)
        ssq = jnp.zeros(inv_s.shape, F32)
        for t in range(n_tiles):
            e = jnp.dot(pb, wp_ref[:, t * bn:(t + 1) * bn].astype(BF16),
                        preferred_element_type=F32)
            e_s[t] = e
            ssq = ssq + jnp.sum(e * e, axis=-1, keepdims=True)
        inv_s[...] = lax.rsqrt(ssq / (n_tiles * bn) + EPS)

    xg = xg_ref[...]
    gate = jax.nn.sigmoid(jnp.dot(xg, wg_ref[...], preferred_element_type=F32)
                          * _inv_rms(ssq_ref, xg.shape[1]))
    e = (e_s[j] * inv_s[...]) * pg_ref[...]
    o = x_ref[...] + gate * e
    o_ref[...] = o
    if emit_next:
        _emit_normed(o, gain_ref, nxg_ref, nssq_ref, j == 0)


def _ple(xg, ssq, w_gate, p, w_proj_stack, layer, post_gain, x, next_gain):
    m, d = x.shape
    pd = p.shape[1]
    bm = _tile(m, 512)
    bn = _tile(d, 1024)
    emit_next = next_gain is not None
    tile = pl.BlockSpec((bm, bn), lambda i, j: (i, j))
    row = pl.BlockSpec((1, bn), lambda i, j: (0, j))
    stats = pl.BlockSpec((bm, LANES), lambda i, j: (i, 0))
    in_specs = [pl.BlockSpec((bm, d), lambda i, j: (i, 0)), stats,
                pl.BlockSpec((d, bn), lambda i, j: (0, j)),
                pl.BlockSpec((bm, pd), lambda i, j: (i, 0)),
                pl.BlockSpec((None, pd, d), lambda i, j: (layer, 0, 0)),
                row, tile]
    args = [xg, ssq, w_gate, p, w_proj_stack, post_gain.reshape(1, d), x]
    out_specs = [tile]
    out_shape = [jax.ShapeDtypeStruct((m, d), F32)]
    if emit_next:
        in_specs.append(row)
        args.append(next_gain.reshape(1, d))
        out_specs += [tile, stats]
        out_shape += [jax.ShapeDtypeStruct((m, d), BF16), jax.ShapeDtypeStruct((m, LANES), F32)]
    return pl.pallas_call(
        functools.partial(_ple_body, emit_next=emit_next),
        grid=(m // bm, d // bn),
        in_specs=in_specs,
        out_specs=out_specs,
        out_shape=out_shape,
        scratch_shapes=[pltpu.VMEM((d // bn, bm, bn), F32),
                        pltpu.VMEM((bm, 1), F32)],
        compiler_params=_params(("parallel", "arbitrary"), 56),
        name="ple",
    )(*args)


def kernel(x, p, ffn1_norm, ffn1_w_gate, ffn1_w_up, ffn1_w_down, mix_norm, w_in, q_norm, k_norm, lambda_q1, lambda_k1, lambda_q2, lambda_k2, attn_subln, conv_w, conv_norm, w_out, ffn2_norm, ffn2_w_gate, ffn2_w_up, ffn2_w_down, ple_w_proj, ple_post_norm, ple_gate_norm, ple_w_gate):
    bsz, seq, d = x.shape
    depth = p.shape[0]
    m = bsz * seq
    attn_width = w_out.shape[1] - conv_w.shape[2]
    conv_width = conv_w.shape[2]
    n_heads = attn_width // V_DIM
    assert w_in.shape[2] == 3 * attn_width + 3 * conv_width

    xf = x.reshape(m, d)
    xg, ssq = _prescale(xf, ffn1_norm[0])
    for i in range(depth):
        lam_init = 0.8 - 0.6 * math.exp(-0.3 * i)
        lam = (jnp.exp(jnp.sum(lambda_q1[i] * lambda_k1[i]))
               - jnp.exp(jnp.sum(lambda_q2[i] * lambda_k2[i])) + lam_init)
        scal = jnp.stack([lam, jnp.asarray(1.0 - lam_init, F32)]).astype(F32)

        xf, xg, ssq = _ffn(xf, xg, ssq, ffn1_w_gate, ffn1_w_up, ffn1_w_down, i, mix_norm[i])

        z = _in_proj(xg, ssq, w_in, i).reshape(bsz, seq, -1)
        attn = _diff_attention(z, scal, q_norm[i], k_norm[i], attn_subln[i], n_heads)
        conv = _gated_conv(z, 3 * attn_width, conv_width, conv_w[i], conv_norm[i])
        xf, xg, ssq = _out_proj(attn.reshape(m, attn_width), conv.reshape(m, conv_width),
                                w_out, i, xf, ffn2_norm[i])

        xf, xg, ssq = _ffn(xf, xg, ssq, ffn2_w_gate, ffn2_w_up, ffn2_w_down, i,
                           ple_gate_norm[i])

        next_gain = ffn1_norm[i + 1] if i + 1 < depth else None
        res = _ple(xg, ssq, _cast_layer(ple_w_gate, i), p[i].reshape(m, -1), ple_w_proj, i,
                   ple_post_norm[i], xf, next_gain)
        xf, xg, ssq = res if next_gain is not None else (res[0], None, None)
    return xf.reshape(bsz, seq, d)
```

```python
import functools
import math

import jax
import jax.numpy as jnp
from jax import lax
from jax.experimental import pallas as pl
from jax.experimental.pallas import tpu as pltpu

F32 = jnp.float32
BF16 = jnp.bfloat16

EPS = 1e-6
HALF_STEP = 0.5
QK_DIM = 64
V_DIM = 2 * QK_DIM
CONV_GROUP_DIM = 128
CONV_K = 3
LANES = 128
NEG_BIG = -1e30
LOG2E = 1.4426950408889634
SLOPE_PIECES = 3
POS_SPLIT = 256
ALIBI_COLS = 2 * SLOPE_PIECES
ONES_ROWS = 16
VT_ROWS = V_DIM + ONES_ROWS
ATTN_BLOCK = 512
ATTN_HEADS_PER_STEP = 4
PREP_BLOCKS = 2

MIB = 1024 * 1024
VMEM_LIMIT_MIB = 56


def _tile(dim, pref):
    t = min(dim, pref)
    while dim % t:
        t //= 2
    return t


def _params(*semantics):
    return pltpu.CompilerParams(dimension_semantics=semantics,
                                vmem_limit_bytes=VMEM_LIMIT_MIB * MIB)


def _emit_normed(o, gain_ref, xg_ref, ssq_ref, first):
    xg_ref[...] = (o * gain_ref[...]).astype(xg_ref.dtype)
    part = jnp.broadcast_to(jnp.sum(o * o, axis=-1, keepdims=True), ssq_ref.shape)

    @pl.when(first)
    def _():
        ssq_ref[...] = part

    @pl.when(jnp.logical_not(first))
    def _():
        ssq_ref[...] += part


def _inv_rms(ssq_ref, width):
    return lax.rsqrt(ssq_ref[:, 0:1] / width + EPS)


def _prescale_body(x_ref, g_ref, xg_ref, ssq_ref):
    x = x_ref[...]
    xg_ref[...] = (x * g_ref[...]).astype(xg_ref.dtype)
    ssq_ref[...] = jnp.broadcast_to(jnp.sum(x * x, axis=-1, keepdims=True), ssq_ref.shape)


def _prescale(x, gain):
    m, d = x.shape
    bm = _tile(m, 512)
    return pl.pallas_call(
        _prescale_body,
        grid=(m // bm,),
        in_specs=[pl.BlockSpec((bm, d), lambda i: (i, 0)),
                  pl.BlockSpec((1, d), lambda i: (0, 0))],
        out_specs=[pl.BlockSpec((bm, d), lambda i: (i, 0)),
                   pl.BlockSpec((bm, LANES), lambda i: (i, 0))],
        out_shape=[jax.ShapeDtypeStruct((m, d), BF16),
                   jax.ShapeDtypeStruct((m, LANES), F32)],
        compiler_params=_params("parallel"),
        name="prescale",
    )(x, gain.reshape(1, d))


def _cast_body(w_ref, o_ref):
    o_ref[...] = w_ref[...].astype(o_ref.dtype)


def _cast_layer(w_stack, layer):
    _, r, c = w_stack.shape
    br = _tile(r, 256)
    return pl.pallas_call(
        _cast_body,
        grid=(r // br,),
        in_specs=[pl.BlockSpec((None, br, c), lambda i: (layer, i, 0))],
        out_specs=pl.BlockSpec((br, c), lambda i: (i, 0)),
        out_shape=jax.ShapeDtypeStruct((r, c), BF16),
        compiler_params=_params("parallel"),
        name="cast_bf16",
    )(w_stack)


def _gate_up_body(xg_ref, ssq_ref, wg_ref, wu_ref, a_ref):
    xg = xg_ref[...]
    inv = _inv_rms(ssq_ref, xg.shape[1])
    g = jnp.dot(xg, wg_ref[...].astype(BF16), preferred_element_type=F32) * inv
    u = jnp.dot(xg, wu_ref[...].astype(BF16), preferred_element_type=F32) * inv
    a_ref[...] = ((g * jax.nn.sigmoid(g)) * u).astype(a_ref.dtype)


def _gate_up(xg, ssq, wg_stack, wu_stack, layer):
    m, d = xg.shape
    f = wg_stack.shape[2]
    bm = _tile(m, 1024)
    bf = _tile(f, 256)
    wspec = pl.BlockSpec((None, d, bf), lambda i, k: (layer, 0, k))
    return pl.pallas_call(
        _gate_up_body,
        grid=(m // bm, f // bf),
        in_specs=[pl.BlockSpec((bm, d), lambda i, k: (i, 0)),
                  pl.BlockSpec((bm, LANES), lambda i, k: (i, 0)), wspec, wspec],
        out_specs=pl.BlockSpec((bm, bf), lambda i, k: (i, k)),
        out_shape=jax.ShapeDtypeStruct((m, f), BF16),
        compiler_params=_params("parallel", "arbitrary"),
        name="ffn_gate_up",
    )(xg, ssq, wg_stack, wu_stack)


def _down_body(a_ref, wd_ref, x_ref, gain_ref, o_ref, xg_ref, ssq_ref):
    o = x_ref[...] + HALF_STEP * jnp.dot(a_ref[...], wd_ref[...], preferred_element_type=F32)
    o_ref[...] = o
    _emit_normed(o, gain_ref, xg_ref, ssq_ref, pl.program_id(1) == 0)


def _down(a, wd, x, next_gain):
    m, d = x.shape
    f = a.shape[1]
    bm = _tile(m, 512)
    bn = _tile(d, 512)
    tile = pl.BlockSpec((bm, bn), lambda i, j: (i, j))
    return pl.pallas_call(
        _down_body,
        grid=(m // bm, d // bn),
        in_specs=[pl.BlockSpec((bm, f), lambda i, j: (i, 0)),
                  pl.BlockSpec((f, bn), lambda i, j: (0, j)),
                  tile,
                  pl.BlockSpec((1, bn), lambda i, j: (0, j))],
        out_specs=[tile, tile, pl.BlockSpec((bm, LANES), lambda i, j: (i, 0))],
        out_shape=[jax.ShapeDtypeStruct((m, d), F32), jax.ShapeDtypeStruct((m, d), BF16),
                   jax.ShapeDtypeStruct((m, LANES), F32)],
        compiler_params=_params("parallel", "arbitrary"),
        name="ffn_down",
    )(a, wd, x, next_gain.reshape(1, d))


def _ffn(x, xg, ssq, wg_stack, wu_stack, wd_stack, layer, next_gain):
    a = _gate_up(xg, ssq, wg_stack, wu_stack, layer)
    return _down(a, _cast_layer(wd_stack, layer), x, next_gain)


def _in_proj_body(xg_ref, ssq_ref, w_ref, o_ref):
    xg = xg_ref[...]
    o_ref[...] = (jnp.dot(xg, w_ref[...].astype(BF16), preferred_element_type=F32)
                  * _inv_rms(ssq_ref, xg.shape[1]))


def _in_proj(xg, ssq, w_stack, layer):
    m, kd = xg.shape
    n = w_stack.shape[2]
    bm = _tile(m, 1024)
    bn = _tile(n, 512)
    return pl.pallas_call(
        _in_proj_body,
        grid=(m // bm, n // bn),
        in_specs=[pl.BlockSpec((bm, kd), lambda i, j: (i, 0)),
                  pl.BlockSpec((bm, LANES), lambda i, j: (i, 0)),
                  pl.BlockSpec((None, kd, bn), lambda i, j: (layer, 0, j))],
        out_specs=pl.BlockSpec((bm, bn), lambda i, j: (i, j)),
        out_shape=jax.ShapeDtypeStruct((m, n), F32),
        compiler_params=_params("parallel", "arbitrary"),
        name="in_proj",
    )(xg, ssq, w_stack)


def _attn_prep_body(q_ref, k_ref, v_ref, qg_ref, kg_ref, qa_ref, qt_ref, kp_ref, vt_ref, *, blk):
    rows = q_ref.shape[0]
    lane = lax.broadcasted_iota(jnp.int32, (1, V_DIM), 1)
    lo = lane < QK_DIM

    def group_norm(t, gain):
        t2 = t * t
        s_lo = jnp.sum(jnp.where(lo, t2, 0.0), axis=-1, keepdims=True)
        s_hi = jnp.sum(jnp.where(lo, 0.0, t2), axis=-1, keepdims=True)
        ms = jnp.where(lo, s_lo, s_hi) * (1.0 / QK_DIM)
        return (t * lax.rsqrt(ms + EPS)) * gain

    qn = group_norm(q_ref[...], qg_ref[...]) * (QK_DIM ** -0.5 * LOG2E)
    qt_ref[0] = jnp.where(lo, qn, qa_ref[0:1, :]).T.astype(BF16)
    qt_ref[1] = jnp.where(lo, qa_ref[1:2, :], qn).T.astype(BF16)

    kn = group_norm(k_ref[...], kg_ref[...])
    pos = lax.broadcasted_iota(jnp.int32, (rows, V_DIM), 0) & (blk - 1)
    pos_lo = (pos & (POS_SPLIT - 1)).astype(F32)
    pos_hi = (pos - (pos & (POS_SPLIT - 1))).astype(F32)
    col = jnp.where(lo, lane + QK_DIM, lane)
    ka = jnp.where((col >= QK_DIM) & (col < QK_DIM + SLOPE_PIECES), pos_lo,
                   jnp.where((col >= QK_DIM + SLOPE_PIECES) & (col < QK_DIM + ALIBI_COLS),
                             pos_hi, 0.0))
    kp_ref[0] = jnp.where(lo, kn, ka).astype(BF16)
    kp_ref[1] = jnp.where(lo, ka, kn).astype(BF16)

    for t in range(rows // blk):
        vt_ref[t, 0:V_DIM, :] = v_ref[t * blk:(t + 1) * blk, :].T.astype(BF16)
        vt_ref[t, V_DIM:, :] = jnp.ones((ONES_ROWS, blk), BF16)


def _attn_prep(z, q_gain, k_gain, q_aug, n_heads, blk):
    bsz, seq, _ = z.shape
    assert blk & (blk - 1) == 0
    nb = seq // blk
    per = PREP_BLOCKS if nb % PREP_BLOCKS == 0 else 1
    rows = per * blk
    gain_spec = pl.BlockSpec((1, V_DIM), lambda b, h, s: (0, 0))
    return pl.pallas_call(
        functools.partial(_attn_prep_body, blk=blk),
        grid=(bsz, n_heads, nb // per),
        in_specs=[pl.BlockSpec((None, rows, V_DIM), lambda b, h, s: (b, s, h)),
                  pl.BlockSpec((None, rows, V_DIM), lambda b, h, s: (b, s, n_heads + h)),
                  pl.BlockSpec((None, rows, V_DIM), lambda b, h, s: (b, s, 2 * n_heads + h)),
                  gain_spec, gain_spec,
                  pl.BlockSpec((None, 2, V_DIM), lambda b, h, s: (h, 0, 0))],
        out_specs=[pl.BlockSpec((None, None, 2, V_DIM, rows), lambda b, h, s: (b, h, 0, 0, s)),
                   pl.BlockSpec((None, None, 2, rows, V_DIM), lambda b, h, s: (b, h, 0, s, 0)),
                   pl.BlockSpec((None, None, per, VT_ROWS, blk), lambda b, h, s: (b, h, s, 0, 0))],
        out_shape=[jax.ShapeDtypeStruct((bsz, n_heads, 2, V_DIM, seq), BF16),
                   jax.ShapeDtypeStruct((bsz, n_heads, 2, seq, V_DIM), BF16),
                   jax.ShapeDtypeStruct((bsz, n_heads, nb, VT_ROWS, blk), BF16)],
        compiler_params=_params("parallel", "parallel", "parallel"),
        name="attn_prep",
    )(z, z, z, jnp.tile(q_gain, 2).reshape(1, V_DIM), jnp.tile(k_gain, 2).reshape(1, V_DIM), q_aug)


def _attn_body(scal_ref, sl_ref, qt_ref, k_ref, vt_ref, sg_ref, o_ref,
               m_s, acc_s, st_a, st_b, *, blk, hps):
    hg = pl.program_id(1)
    qi = pl.program_id(2)
    n_chains = 2 * hps
    m_s[...] = jnp.full_like(m_s, NEG_BIG)
    acc_s[...] = jnp.zeros_like(acc_s)
    st_bufs = (st_a, st_b)

    def scores(j, dst):
        r = pl.multiple_of(j * blk, blk)
        for c in range(n_chains):
            st_bufs[dst][c] = jnp.dot(k_ref[c // 2, c % 2, pl.ds(r, blk), :], qt_ref[c // 2, c % 2],
                                      preferred_element_type=F32)

    def absorb(j, src, diagonal):
        for c in range(n_chains):
            off = sl_ref[hg * hps + c // 2] * ((j - qi) * blk).astype(F32)
            st = st_bufs[src][c]
            if diagonal:
                key = lax.broadcasted_iota(jnp.int32, st.shape, 0)
                qry = lax.broadcasted_iota(jnp.int32, st.shape, 1)
                st = jnp.where(key <= qry, st, -jnp.inf)
            m_old = m_s[c]
            m_new = jnp.maximum(m_old, jnp.max(st, axis=0, keepdims=True) + off)
            alpha = jnp.exp2(m_old - m_new)
            pt = jnp.exp2(st - (m_new - off)).astype(BF16)
            acc_s[c] = alpha * acc_s[c] + jnp.dot(vt_ref[c // 2, j], pt,
                                                  preferred_element_type=F32)
            m_s[c] = m_new

    scores(0, 0)

    def pair(t, carry):
        j = 2 * t
        scores(j + 1, 1)
        absorb(j, 0, False)
        scores(j + 2, 0)
        absorb(j + 1, 1, False)
        return carry
    lax.fori_loop(0, qi // 2, pair, 0)

    @pl.when(qi % 2 == 0)
    def _():
        absorb(qi, 0, True)

    @pl.when(qi % 2 == 1)
    def _():
        scores(qi, 1)
        absorb(qi - 1, 0, False)
        absorb(qi, 1, True)

    lam = scal_ref[0]
    for hh in range(hps):
        a0 = acc_s[2 * hh]
        a1 = acc_s[2 * hh + 1]
        o = a0[:V_DIM] / a0[V_DIM:V_DIM + 1] - lam * (a1[:V_DIM] / a1[V_DIM:V_DIM + 1])
        ms = jnp.mean(o * o, axis=0, keepdims=True)
        y = ((o * lax.rsqrt(ms + EPS)) * sg_ref[...]) * scal_ref[1]
        o_ref[:, hh * V_DIM:(hh + 1) * V_DIM] = y.T.astype(o_ref.dtype)


def _bf16_part(x):
    bits = lax.bitcast_convert_type(x, jnp.uint32) & jnp.uint32(0xFFFF0000)
    return lax.bitcast_convert_type(bits, F32)


def _alibi_columns(n_heads):
    slopes = jnp.exp2(-8.0 * jnp.arange(1, n_heads + 1, dtype=F32) / n_heads)
    sl = slopes * LOG2E
    a1 = _bf16_part(sl)
    a2 = _bf16_part(sl - a1)
    a3 = _bf16_part(sl - a1 - a2)
    pieces = jnp.stack([a1, a2, a3, a1, a2, a3], axis=1)
    q_aug = jnp.zeros((n_heads, 2, V_DIM), F32)
    q_aug = q_aug.at[:, 0, QK_DIM:QK_DIM + ALIBI_COLS].set(pieces)
    q_aug = q_aug.at[:, 1, 0:ALIBI_COLS].set(pieces)
    return a1 + a2 + a3, q_aug


def _diff_attention(z, scal, q_gain, k_gain, subln_gain, n_heads):
    bsz, seq, _ = z.shape
    blk = _tile(seq, ATTN_BLOCK)
    sl, q_aug = _alibi_columns(n_heads)
    qt, kp, vt = _attn_prep(z, q_gain, k_gain, q_aug, n_heads, blk)
    hps = ATTN_HEADS_PER_STEP if n_heads % ATTN_HEADS_PER_STEP == 0 else 1
    n_chains = 2 * hps
    smem = pl.BlockSpec(memory_space=pltpu.SMEM)
    return pl.pallas_call(
        functools.partial(_attn_body, blk=blk, hps=hps),
        grid=(bsz, n_heads // hps, seq // blk),
        in_specs=[smem, smem,
                  pl.BlockSpec((None, hps, 2, V_DIM, blk), lambda b, h, i: (b, h, 0, 0, i)),
                  pl.BlockSpec((None, hps, 2, seq, V_DIM), lambda b, h, i: (b, h, 0, 0, 0),
                               pipeline_mode=pl.Buffered(1)),
                  pl.BlockSpec((None, hps, seq // blk, VT_ROWS, blk), lambda b, h, i: (b, h, 0, 0, 0),
                               pipeline_mode=pl.Buffered(1)),
                  pl.BlockSpec((V_DIM, 1), lambda b, h, i: (0, 0))],
        out_specs=pl.BlockSpec((None, blk, hps * V_DIM), lambda b, h, i: (b, i, h)),
        out_shape=jax.ShapeDtypeStruct((bsz, seq, n_heads * V_DIM), BF16),
        scratch_shapes=[pltpu.VMEM((n_chains, 1, blk), F32),
                        pltpu.VMEM((n_chains, VT_ROWS, blk), F32),
                        pltpu.VMEM((n_chains, blk, blk), F32),
                        pltpu.VMEM((n_chains, blk, blk), F32)],
        compiler_params=_params("parallel", "parallel", "arbitrary"),
        name="diff_attn",
    )(scal, sl, qt, kp, vt, subln_gain.reshape(V_DIM, 1))


def _conv_body(b_ref, c_ref, u_ref, w_ref, g_ref, o_ref, tail_s):
    si = pl.program_id(2)
    rows, cols = o_ref.shape

    @pl.when(si == 0)
    def _():
        tail_s[...] = jnp.zeros_like(tail_s)

    cu = c_ref[...] * u_ref[...]
    row = lax.broadcasted_iota(jnp.int32, (rows, cols), 0)
    tail = tail_s[...]
    prev1 = jnp.where(row == 0, tail[7:8, :], pltpu.roll(cu, 1, axis=0))
    prev2 = jnp.where(row == 0, tail[6:7, :],
                      jnp.where(row == 1, tail[7:8, :], pltpu.roll(cu, 2, axis=0)))
    acc = prev2 * w_ref[0:1, :]
    acc = acc + prev1 * w_ref[1:2, :]
    acc = acc + cu * w_ref[2:3, :]
    y = b_ref[...] * acc
    tail_s[...] = cu[rows - 8:, :]
    for gi in range(cols // CONV_GROUP_DIM):
        sl = slice(gi * CONV_GROUP_DIM, (gi + 1) * CONV_GROUP_DIM)
        yg = y[:, sl]
        ms = jnp.mean(yg * yg, axis=-1, keepdims=True)
        o_ref[:, sl] = ((yg * lax.rsqrt(ms + EPS)) * g_ref[...]).astype(o_ref.dtype)


def _gated_conv(z, col0, width, conv_w, group_gain):
    bsz, seq, _ = z.shape
    cb = _tile(width, 512)
    bs = _tile(seq, 1024)
    nb = width // cb
    base = col0 // cb
    zspec = lambda slab: pl.BlockSpec(
        (None, bs, cb), lambda b, c, s: (b, s, base + slab * nb + c))
    return pl.pallas_call(
        _conv_body,
        grid=(bsz, nb, seq // bs),
        in_specs=[zspec(0), zspec(1), zspec(2),
                  pl.BlockSpec((CONV_K, cb), lambda b, c, s: (0, c)),
                  pl.BlockSpec((1, CONV_GROUP_DIM), lambda b, c, s: (0, 0))],
        out_specs=pl.BlockSpec((None, bs, cb), lambda b, c, s: (b, s, c)),
        out_shape=jax.ShapeDtypeStruct((bsz, seq, width), BF16),
        scratch_shapes=[pltpu.VMEM((8, cb), F32)],
        compiler_params=_params("parallel", "parallel", "arbitrary"),
        name="gated_conv",
    )(z, z, z, conv_w, group_gain.reshape(1, CONV_GROUP_DIM))


def _out_proj_body(a_ref, y_ref, wa_ref, wc_ref, x_ref, gain_ref, o_ref, xg_ref, ssq_ref):
    mix = (jnp.dot(a_ref[...], wa_ref[...].astype(BF16), preferred_element_type=F32)
           + jnp.dot(y_ref[...], wc_ref[...].astype(BF16), preferred_element_type=F32))
    o = x_ref[...] + mix
    o_ref[...] = o
    _emit_normed(o, gain_ref, xg_ref, ssq_ref, pl.program_id(1) == 0)


def _out_proj(attn, conv, w_stack, layer, x, next_gain):
    m, d = x.shape
    bm = _tile(m, 1024)
    bn = _tile(d, 512)
    ka, kc = attn.shape[1], conv.shape[1]
    assert ka == kc, "row blocks of the output projection are indexed in units of one head group"
    tile = pl.BlockSpec((bm, bn), lambda i, j: (i, j))
    return pl.pallas_call(
        _out_proj_body,
        grid=(m // bm, d // bn),
        in_specs=[pl.BlockSpec((bm, ka), lambda i, j: (i, 0)),
                  pl.BlockSpec((bm, kc), lambda i, j: (i, 0)),
                  pl.BlockSpec((None, ka, bn), lambda i, j: (layer, 0, j)),
                  pl.BlockSpec((None, kc, bn), lambda i, j: (layer, 1, j)),
                  tile,
                  pl.BlockSpec((1, bn), lambda i, j: (0, j))],
        out_specs=[tile, tile, pl.BlockSpec((bm, LANES), lambda i, j: (i, 0))],
        out_shape=[jax.ShapeDtypeStruct((m, d), F32), jax.ShapeDtypeStruct((m, d), BF16),
                   jax.ShapeDtypeStruct((m, LANES), F32)],
        compiler_params=_params("parallel", "arbitrary"),
        name="out_proj",
    )(attn, conv, w_stack, w_stack, x, next_gain.reshape(1, d))


def _ple_body(xg_ref, ssq_ref, wg_ref, p_ref, wp_ref, pg_ref, x_ref, *rest, emit_next):
    if emit_next:
        gain_ref, o_ref, nxg_ref, nssq_ref, e_s, inv_s = rest
    else:
        o_ref, e_s, inv_s = rest
    j = pl.program_id(1)
    n_tiles, _, bn = e_s.shape

    @pl.when(j == 0)
    def _():
        pb = p_ref[...].astype(BF16)
        ssq = jnp.zeros(inv_s.shape, F32)
        for t in range(n_tiles):
            e = jnp.dot(pb, wp_ref[:, t * bn:(t + 1) * bn].astype(BF16),
                        preferred_element_type=F32)
            e_s[t] = e
            ssq = ssq + jnp.sum(e * e, axis=-1, keepdims=True)
        inv_s[...] = lax.rsqrt(ssq / (n_tiles * bn) + EPS)

    xg = xg_ref[...]
    gate = jax.nn.sigmoid(jnp.dot(xg, wg_ref[...], preferred_element_type=F32)
                          * _inv_rms(ssq_ref, xg.shape[1]))
    e = (e_s[j] * inv_s[...]) * pg_ref[...]
    o = x_ref[...] + gate * e
    o_ref[...] = o
    if emit_next:
        _emit_normed(o, gain_ref, nxg_ref, nssq_ref, j == 0)


def _ple(xg, ssq, w_gate, p, w_proj_stack, layer, post_gain, x, next_gain):
    m, d = x.shape
    pd = p.shape[1]
    bm = _tile(m, 512)
    bn = _tile(d, 1024)
    emit_next = next_gain is not None
    tile = pl.BlockSpec((bm, bn), lambda i, j: (i, j))
    row = pl.BlockSpec((1, bn), lambda i, j: (0, j))
    stats = pl.BlockSpec((bm, LANES), lambda i, j: (i, 0))
    in_specs = [pl.BlockSpec((bm, d), lambda i, j: (i, 0)), stats,
                pl.BlockSpec((d, bn), lambda i, j: (0, j)),
                pl.BlockSpec((bm, pd), lambda i, j: (i, 0)),
                pl.BlockSpec((None, pd, d), lambda i, j: (layer, 0, 0)),
                row, tile]
    args = [xg, ssq, w_gate, p, w_proj_stack, post_gain.reshape(1, d), x]
    out_specs = [tile]
    out_shape = [jax.ShapeDtypeStruct((m, d), F32)]
    if emit_next:
        in_specs.append(row)
        args.append(next_gain.reshape(1, d))
        out_specs += [tile, stats]
        out_shape += [jax.ShapeDtypeStruct((m, d), BF16), jax.ShapeDtypeStruct((m, LANES), F32)]
    return pl.pallas_call(
        functools.partial(_ple_body, emit_next=emit_next),
        grid=(m // bm, d // bn),
        in_specs=in_specs,
        out_specs=out_specs,
        out_shape=out_shape,
        scratch_shapes=[pltpu.VMEM((d // bn, bm, bn), F32),
                        pltpu.VMEM((bm, 1), F32)],
        compiler_params=_params("parallel", "arbitrary"),
        name="ple",
    )(*args)


def kernel(x, p, ffn1_norm, ffn1_w_gate, ffn1_w_up, ffn1_w_down, mix_norm, w_in, q_norm, k_norm, lambda_q1, lambda_k1, lambda_q2, lambda_k2, attn_subln, conv_w, conv_norm, w_out, ffn2_norm, ffn2_w_gate, ffn2_w_up, ffn2_w_down, ple_w_proj, ple_post_norm, ple_gate_norm, ple_w_gate):
    bsz, seq, d = x.shape
    depth = p.shape[0]
    m = bsz * seq
    attn_width = w_out.shape[1] - conv_w.shape[2]
    conv_width = conv_w.shape[2]
    n_heads = attn_width // V_DIM
    assert w_in.shape[2] == 3 * attn_width + 3 * conv_width

    xf = x.reshape(m, d)
    xg, ssq = _prescale(xf, ffn1_norm[0])
    for i in range(depth):
        lam_init = 0.8 - 0.6 * math.exp(-0.3 * i)
        lam = (jnp.exp(jnp.sum(lambda_q1[i] * lambda_k1[i]))
               - jnp.exp(jnp.sum(lambda_q2[i] * lambda_k2[i])) + lam_init)
        scal = jnp.stack([lam, jnp.asarray(1.0 - lam_init, F32)]).astype(F32)

        xf, xg, ssq = _ffn(xf, xg, ssq, ffn1_w_gate, ffn1_w_up, ffn1_w_down, i, mix_norm[i])

        z = _in_proj(xg, ssq, w_in, i).reshape(bsz, seq, -1)
        attn = _diff_attention(z, scal, q_norm[i], k_norm[i], attn_subln[i], n_heads)
        conv = _gated_conv(z, 3 * attn_width, conv_width, conv_w[i], conv_norm[i])
        xf, xg, ssq = _out_proj(attn.reshape(m, attn_width), conv.reshape(m, conv_width),
                                w_out, i, xf, ffn2_norm[i])

        xf, xg, ssq = _ffn(xf, xg, ssq, ffn2_w_gate, ffn2_w_up, ffn2_w_down, i,
                           ple_gate_norm[i])

        next_gain = ffn1_norm[i + 1] if i + 1 < depth else None
        res = _ple(xg, ssq, _cast_layer(ple_w_gate, i), p[i].reshape(m, -1), ple_w_proj, i,
                   ple_post_norm[i], xf, next_gain)
        xf, xg, ssq = res if next_gain is not None else (res[0], None, None)
    return xf.reshape(bsz, seq, d)
```

```python
import functools
import math

import jax
import jax.numpy as jnp
from jax import lax
from jax.experimental import pallas as pl
from jax.experimental.pallas import tpu as pltpu

F32 = jnp.float32
BF16 = jnp.bfloat16

EPS = 1e-6
HALF_STEP = 0.5
QK_DIM = 64
V_DIM = 2 * QK_DIM
CONV_GROUP_DIM = 128
CONV_K = 3
LANES = 128
NEG_BIG = -1e30
LOG2E = 1.4426950408889634
SLOPE_PIECES = 3
POS_SPLIT = 256
ALIBI_COLS = 2 * SLOPE_PIECES
ONES_ROWS = 16
VT_ROWS = V_DIM + ONES_ROWS
ATTN_BLOCK = 512
ATTN_HEADS_PER_STEP = 4
PREP_BLOCKS = 2

MIB = 1024 * 1024
VMEM_LIMIT_MIB = 56


def _tile(dim, pref):
    t = min(dim, pref)
    while dim % t:
        t //= 2
    return t


def _params(*semantics):
    return pltpu.CompilerParams(dimension_semantics=semantics,
                                vmem_limit_bytes=VMEM_LIMIT_MIB * MIB)


def _emit_normed(o, gain_ref, xg_ref, ssq_ref, first):
    xg_ref[...] = (o * gain_ref[...]).astype(xg_ref.dtype)
    part = jnp.broadcast_to(jnp.sum(o * o, axis=-1, keepdims=True), ssq_ref.shape)

    @pl.when(first)
    def _():
        ssq_ref[...] = part

    @pl.when(jnp.logical_not(first))
    def _():
        ssq_ref[...] += part


def _inv_rms(ssq_ref, width):
    return lax.rsqrt(ssq_ref[:, 0:1] / width + EPS)


def _prescale_body(x_ref, g_ref, xg_ref, ssq_ref):
    x = x_ref[...]
    xg_ref[...] = (x * g_ref[...]).astype(xg_ref.dtype)
    ssq_ref[...] = jnp.broadcast_to(jnp.sum(x * x, axis=-1, keepdims=True), ssq_ref.shape)


def _prescale(x, gain):
    m, d = x.shape
    bm = _tile(m, 512)
    return pl.pallas_call(
        _prescale_body,
        grid=(m // bm,),
        in_specs=[pl.BlockSpec((bm, d), lambda i: (i, 0)),
                  pl.BlockSpec((1, d), lambda i: (0, 0))],
        out_specs=[pl.BlockSpec((bm, d), lambda i: (i, 0)),
                   pl.BlockSpec((bm, LANES), lambda i: (i, 0))],
        out_shape=[jax.ShapeDtypeStruct((m, d), BF16),
                   jax.ShapeDtypeStruct((m, LANES), F32)],
        compiler_params=_params("parallel"),
        name="prescale",
    )(x, gain.reshape(1, d))


def _cast_body(w_ref, o_ref):
    o_ref[...] = w_ref[...].astype(o_ref.dtype)


def _cast_layer(w_stack, layer):
    _, r, c = w_stack.shape
    br = _tile(r, 256)
    return pl.pallas_call(
        _cast_body,
        grid=(r // br,),
        in_specs=[pl.BlockSpec((None, br, c), lambda i: (layer, i, 0))],
        out_specs=pl.BlockSpec((br, c), lambda i: (i, 0)),
        out_shape=jax.ShapeDtypeStruct((r, c), BF16),
        compiler_params=_params("parallel"),
        name="cast_bf16",
    )(w_stack)


def _gate_up_body(xg_ref, ssq_ref, wg_ref, wu_ref, a_ref):
    xg = xg_ref[...]
    inv = _inv_rms(ssq_ref, xg.shape[1])
    g = jnp.dot(xg, wg_ref[...].astype(BF16), preferred_element_type=F32) * inv
    u = jnp.dot(xg, wu_ref[...].astype(BF16), preferred_element_type=F32) * inv
    a_ref[...] = ((g * jax.nn.sigmoid(g)) * u).astype(a_ref.dtype)


def _gate_up(xg, ssq, wg_stack, wu_stack, layer):
    m, d = xg.shape
    f = wg_stack.shape[2]
    bm = _tile(m, 1024)
    bf = _tile(f, 256)
    wspec = pl.BlockSpec((None, d, bf), lambda i, k: (layer, 0, k))
    return pl.pallas_call(
        _gate_up_body,
        grid=(m // bm, f // bf),
        in_specs=[pl.BlockSpec((bm, d), lambda i, k: (i, 0)),
                  pl.BlockSpec((bm, LANES), lambda i, k: (i, 0)), wspec, wspec],
        out_specs=pl.BlockSpec((bm, bf), lambda i, k: (i, k)),
        out_shape=jax.ShapeDtypeStruct((m, f), BF16),
        compiler_params=_params("parallel", "arbitrary"),
        name="ffn_gate_up",
    )(xg, ssq, wg_stack, wu_stack)


def _down_body(a_ref, wd_ref, x_ref, gain_ref, o_ref, xg_ref, ssq_ref):
    o = x_ref[...] + HALF_STEP * jnp.dot(a_ref[...], wd_ref[...], preferred_element_type=F32)
    o_ref[...] = o
    _emit_normed(o, gain_ref, xg_ref, ssq_ref, pl.program_id(1) == 0)


def _down(a, wd, x, next_gain):
    m, d = x.shape
    f = a.shape[1]
    bm = _tile(m, 512)
    bn = _tile(d, 512)
    tile = pl.BlockSpec((bm, bn), lambda i, j: (i, j))
    return pl.pallas_call(
        _down_body,
        grid=(m // bm, d // bn),
        in_specs=[pl.BlockSpec((bm, f), lambda i, j: (i, 0)),
                  pl.BlockSpec((f, bn), lambda i, j: (0, j)),
                  tile,
                  pl.BlockSpec((1, bn), lambda i, j: (0, j))],
        out_specs=[tile, tile, pl.BlockSpec((bm, LANES), lambda i, j: (i, 0))],
        out_shape=[jax.ShapeDtypeStruct((m, d), F32), jax.ShapeDtypeStruct((m, d), BF16),
                   jax.ShapeDtypeStruct((m, LANES), F32)],
        compiler_params=_params("parallel", "arbitrary"),
        name="ffn_down",
    )(a, wd, x, next_gain.reshape(1, d))


def _ffn(x, xg, ssq, wg_stack, wu_stack, wd_stack, layer, next_gain):
    a = _gate_up(xg, ssq, wg_stack, wu_stack, layer)
    return _down(a, _cast_layer(wd_stack, layer), x, next_gain)


def _in_proj_body(xg_ref, ssq_ref, w_ref, o_ref):
    xg = xg_ref[...]
    o_ref[...] = (jnp.dot(xg, w_ref[...].astype(BF16), preferred_element_type=F32)
                  * _inv_rms(ssq_ref, xg.shape[1]))


def _in_proj(xg, ssq, w_stack, layer):
    m, kd = xg.shape
    n = w_stack.shape[2]
    bm = _tile(m, 1024)
    bn = _tile(n, 512)
    return pl.pallas_call(
        _in_proj_body,
        grid=(m // bm, n // bn),
        in_specs=[pl.BlockSpec((bm, kd), lambda i, j: (i, 0)),
                  pl.BlockSpec((bm, LANES), lambda i, j: (i, 0)),
                  pl.BlockSpec((None, kd, bn), lambda i, j: (layer, 0, j))],
        out_specs=pl.BlockSpec((bm, bn), lambda i, j: (i, j)),
        out_shape=jax.ShapeDtypeStruct((m, n), F32),
        compiler_params=_params("parallel", "arbitrary"),
        name="in_proj",
    )(xg, ssq, w_stack)


def _mixer_prep_body(q_ref, k_ref, v_ref, qg_ref, kg_ref, qa_ref, b_ref, c_ref, u_ref, cw_ref,
                     cg_ref, qt_ref, kp_ref, vt_ref, y_ref, tail_s, *, blk):
    _gated_conv_tile(b_ref[...], c_ref[...], u_ref[...], cw_ref, cg_ref, y_ref, tail_s,
                     pl.program_id(2) == 0)
    rows = q_ref.shape[0]
    lane = lax.broadcasted_iota(jnp.int32, (1, V_DIM), 1)
    lo = lane < QK_DIM

    def group_norm(t, gain):
        t2 = t * t
        s_lo = jnp.sum(jnp.where(lo, t2, 0.0), axis=-1, keepdims=True)
        s_hi = jnp.sum(jnp.where(lo, 0.0, t2), axis=-1, keepdims=True)
        ms = jnp.where(lo, s_lo, s_hi) * (1.0 / QK_DIM)
        return (t * lax.rsqrt(ms + EPS)) * gain

    qn = group_norm(q_ref[...], qg_ref[...]) * (QK_DIM ** -0.5 * LOG2E)
    qt_ref[0] = jnp.where(lo, qn, qa_ref[0:1, :]).T.astype(BF16)
    qt_ref[1] = jnp.where(lo, qa_ref[1:2, :], qn).T.astype(BF16)

    kn = group_norm(k_ref[...], kg_ref[...])
    pos = lax.broadcasted_iota(jnp.int32, (rows, V_DIM), 0) & (blk - 1)
    pos_lo = (pos & (POS_SPLIT - 1)).astype(F32)
    pos_hi = (pos - (pos & (POS_SPLIT - 1))).astype(F32)
    col = jnp.where(lo, lane + QK_DIM, lane)
    ka = jnp.where((col >= QK_DIM) & (col < QK_DIM + SLOPE_PIECES), pos_lo,
                   jnp.where((col >= QK_DIM + SLOPE_PIECES) & (col < QK_DIM + ALIBI_COLS),
                             pos_hi, 0.0))
    kp_ref[0] = jnp.where(lo, kn, ka).astype(BF16)
    kp_ref[1] = jnp.where(lo, ka, kn).astype(BF16)

    for t in range(rows // blk):
        vt_ref[t, 0:V_DIM, :] = v_ref[t * blk:(t + 1) * blk, :].T.astype(BF16)
        vt_ref[t, V_DIM:, :] = jnp.ones((ONES_ROWS, blk), BF16)


def _mixer_prep(z, q_gain, k_gain, q_aug, conv_w, conv_gain, n_heads, blk):
    bsz, seq, _ = z.shape
    slab = lambda k: pl.BlockSpec((None, rows, V_DIM), lambda b, h, s: (b, s, k * n_heads + h))
    assert blk & (blk - 1) == 0
    nb = seq // blk
    per = PREP_BLOCKS if nb % PREP_BLOCKS == 0 else 1
    rows = per * blk
    gain_spec = pl.BlockSpec((1, V_DIM), lambda b, h, s: (0, 0))
    return pl.pallas_call(
        functools.partial(_mixer_prep_body, blk=blk),
        grid=(bsz, n_heads, nb // per),
        in_specs=[slab(0), slab(1), slab(2), gain_spec, gain_spec,
                  pl.BlockSpec((None, 2, V_DIM), lambda b, h, s: (h, 0, 0)),
                  slab(3), slab(4), slab(5),
                  pl.BlockSpec((CONV_K, V_DIM), lambda b, h, s: (0, h)),
                  gain_spec],
        out_specs=[pl.BlockSpec((None, None, 2, V_DIM, rows), lambda b, h, s: (b, h, 0, 0, s)),
                   pl.BlockSpec((None, None, 2, rows, V_DIM), lambda b, h, s: (b, h, 0, s, 0)),
                   pl.BlockSpec((None, None, per, VT_ROWS, blk), lambda b, h, s: (b, h, s, 0, 0)),
                   pl.BlockSpec((None, rows, V_DIM), lambda b, h, s: (b, s, h))],
        out_shape=[jax.ShapeDtypeStruct((bsz, n_heads, 2, V_DIM, seq), BF16),
                   jax.ShapeDtypeStruct((bsz, n_heads, 2, seq, V_DIM), BF16),
                   jax.ShapeDtypeStruct((bsz, n_heads, nb, VT_ROWS, blk), BF16),
                   jax.ShapeDtypeStruct((bsz, seq, n_heads * V_DIM), BF16)],
        scratch_shapes=[pltpu.VMEM((8, V_DIM), F32)],
        compiler_params=_params("parallel", "parallel", "arbitrary"),
        name="mixer_prep",
    )(z, z, z, jnp.tile(q_gain, 2).reshape(1, V_DIM), jnp.tile(k_gain, 2).reshape(1, V_DIM), q_aug,
      z, z, z, conv_w, conv_gain.reshape(1, CONV_GROUP_DIM))


def _attn_body(scal_ref, sl_ref, qt_ref, k_ref, vt_ref, sg_ref, o_ref,
               m_s, acc_s, st_a, st_b, *, blk, hps):
    hg = pl.program_id(1)
    qi = pl.program_id(2)
    n_chains = 2 * hps
    m_s[...] = jnp.full_like(m_s, NEG_BIG)
    acc_s[...] = jnp.zeros_like(acc_s)
    st_bufs = (st_a, st_b)

    def scores(j, dst):
        r = pl.multiple_of(j * blk, blk)
        for c in range(n_chains):
            st_bufs[dst][c] = jnp.dot(k_ref[c // 2, c % 2, pl.ds(r, blk), :], qt_ref[c // 2, c % 2],
                                      preferred_element_type=F32)

    def absorb(j, src, diagonal):
        for c in range(n_chains):
            off = sl_ref[hg * hps + c // 2] * ((j - qi) * blk).astype(F32)
            st = st_bufs[src][c]
            if diagonal:
                key = lax.broadcasted_iota(jnp.int32, st.shape, 0)
                qry = lax.broadcasted_iota(jnp.int32, st.shape, 1)
                st = jnp.where(key <= qry, st, -jnp.inf)
            m_old = m_s[c]
            m_new = jnp.maximum(m_old, jnp.max(st, axis=0, keepdims=True) + off)
            alpha = jnp.exp2(m_old - m_new)
            pt = jnp.exp2(st - (m_new - off)).astype(BF16)
            acc_s[c] = alpha * acc_s[c] + jnp.dot(vt_ref[c // 2, j], pt,
                                                  preferred_element_type=F32)
            m_s[c] = m_new

    scores(0, 0)

    def pair(t, carry):
        j = 2 * t
        scores(j + 1, 1)
        absorb(j, 0, False)
        scores(j + 2, 0)
        absorb(j + 1, 1, False)
        return carry
    lax.fori_loop(0, qi // 2, pair, 0)

    @pl.when(qi % 2 == 0)
    def _():
        absorb(qi, 0, True)

    @pl.when(qi % 2 == 1)
    def _():
        scores(qi, 1)
        absorb(qi - 1, 0, False)
        absorb(qi, 1, True)

    lam = scal_ref[0]
    for hh in range(hps):
        a0 = acc_s[2 * hh]
        a1 = acc_s[2 * hh + 1]
        o = a0[:V_DIM] / a0[V_DIM:V_DIM + 1] - lam * (a1[:V_DIM] / a1[V_DIM:V_DIM + 1])
        ms = jnp.mean(o * o, axis=0, keepdims=True)
        y = ((o * lax.rsqrt(ms + EPS)) * sg_ref[...]) * scal_ref[1]
        o_ref[:, hh * V_DIM:(hh + 1) * V_DIM] = y.T.astype(o_ref.dtype)


def _bf16_part(x):
    bits = lax.bitcast_convert_type(x, jnp.uint32) & jnp.uint32(0xFFFF0000)
    return lax.bitcast_convert_type(bits, F32)


def _alibi_columns(n_heads):
    slopes = jnp.exp2(-8.0 * jnp.arange(1, n_heads + 1, dtype=F32) / n_heads)
    sl = slopes * LOG2E
    a1 = _bf16_part(sl)
    a2 = _bf16_part(sl - a1)
    a3 = _bf16_part(sl - a1 - a2)
    pieces = jnp.stack([a1, a2, a3, a1, a2, a3], axis=1)
    q_aug = jnp.zeros((n_heads, 2, V_DIM), F32)
    q_aug = q_aug.at[:, 0, QK_DIM:QK_DIM + ALIBI_COLS].set(pieces)
    q_aug = q_aug.at[:, 1, 0:ALIBI_COLS].set(pieces)
    return a1 + a2 + a3, q_aug


def _diff_attention(qt, kp, vt, sl, scal, subln_gain):
    bsz, n_heads, _, _, seq = qt.shape
    blk = vt.shape[-1]
    hps = ATTN_HEADS_PER_STEP if n_heads % ATTN_HEADS_PER_STEP == 0 else 1
    n_chains = 2 * hps
    smem = pl.BlockSpec(memory_space=pltpu.SMEM)
    return pl.pallas_call(
        functools.partial(_attn_body, blk=blk, hps=hps),
        grid=(bsz, n_heads // hps, seq // blk),
        in_specs=[smem, smem,
                  pl.BlockSpec((None, hps, 2, V_DIM, blk), lambda b, h, i: (b, h, 0, 0, i)),
                  pl.BlockSpec((None, hps, 2, seq, V_DIM), lambda b, h, i: (b, h, 0, 0, 0),
                               pipeline_mode=pl.Buffered(1)),
                  pl.BlockSpec((None, hps, seq // blk, VT_ROWS, blk), lambda b, h, i: (b, h, 0, 0, 0),
                               pipeline_mode=pl.Buffered(1)),
                  pl.BlockSpec((V_DIM, 1), lambda b, h, i: (0, 0))],
        out_specs=pl.BlockSpec((None, blk, hps * V_DIM), lambda b, h, i: (b, i, h)),
        out_shape=jax.ShapeDtypeStruct((bsz, seq, n_heads * V_DIM), BF16),
        scratch_shapes=[pltpu.VMEM((n_chains, 1, blk), F32),
                        pltpu.VMEM((n_chains, VT_ROWS, blk), F32),
                        pltpu.VMEM((n_chains, blk, blk), F32),
                        pltpu.VMEM((n_chains, blk, blk), F32)],
        compiler_params=_params("parallel", "parallel", "arbitrary"),
        name="diff_attn",
    )(scal, sl, qt, kp, vt, subln_gain.reshape(V_DIM, 1))


def _gated_conv_tile(b, c, u, w_ref, g_ref, o_ref, tail_s, first):
    rows, cols = o_ref.shape

    @pl.when(first)
    def _():
        tail_s[...] = jnp.zeros_like(tail_s)

    cu = c * u
    row = lax.broadcasted_iota(jnp.int32, (rows, cols), 0)
    tail = tail_s[...]
    prev1 = jnp.where(row == 0, tail[7:8, :], pltpu.roll(cu, 1, axis=0))
    prev2 = jnp.where(row == 0, tail[6:7, :],
                      jnp.where(row == 1, tail[7:8, :], pltpu.roll(cu, 2, axis=0)))
    acc = prev2 * w_ref[0:1, :]
    acc = acc + prev1 * w_ref[1:2, :]
    acc = acc + cu * w_ref[2:3, :]
    y = b * acc
    tail_s[...] = cu[rows - 8:, :]
    for gi in range(cols // CONV_GROUP_DIM):
        sl = slice(gi * CONV_GROUP_DIM, (gi + 1) * CONV_GROUP_DIM)
        yg = y[:, sl]
        ms = jnp.mean(yg * yg, axis=-1, keepdims=True)
        o_ref[:, sl] = ((yg * lax.rsqrt(ms + EPS)) * g_ref[...]).astype(o_ref.dtype)


def _out_proj_body(a_ref, y_ref, wa_ref, wc_ref, x_ref, gain_ref, o_ref, xg_ref, ssq_ref):
    mix = (jnp.dot(a_ref[...], wa_ref[...].astype(BF16), preferred_element_type=F32)
           + jnp.dot(y_ref[...], wc_ref[...].astype(BF16), preferred_element_type=F32))
    o = x_ref[...] + mix
    o_ref[...] = o
    _emit_normed(o, gain_ref, xg_ref, ssq_ref, pl.program_id(1) == 0)


def _out_proj(attn, conv, w_stack, layer, x, next_gain):
    m, d = x.shape
    bm = _tile(m, 1024)
    bn = _tile(d, 512)
    ka, kc = attn.shape[1], conv.shape[1]
    assert ka == kc, "row blocks of the output projection are indexed in units of one head group"
    tile = pl.BlockSpec((bm, bn), lambda i, j: (i, j))
    return pl.pallas_call(
        _out_proj_body,
        grid=(m // bm, d // bn),
        in_specs=[pl.BlockSpec((bm, ka), lambda i, j: (i, 0)),
                  pl.BlockSpec((bm, kc), lambda i, j: (i, 0)),
                  pl.BlockSpec((None, ka, bn), lambda i, j: (layer, 0, j)),
                  pl.BlockSpec((None, kc, bn), lambda i, j: (layer, 1, j)),
                  tile,
                  pl.BlockSpec((1, bn), lambda i, j: (0, j))],
        out_specs=[tile, tile, pl.BlockSpec((bm, LANES), lambda i, j: (i, 0))],
        out_shape=[jax.ShapeDtypeStruct((m, d), F32), jax.ShapeDtypeStruct((m, d), BF16),
                   jax.ShapeDtypeStruct((m, LANES), F32)],
        compiler_params=_params("parallel", "arbitrary"),
        name="out_proj",
    )(attn, conv, w_stack, w_stack, x, next_gain.reshape(1, d))


def _ple_body(xg_ref, ssq_ref, wg_ref, p_ref, wp_ref, pg_ref, x_ref, *rest, emit_next):
    if emit_next:
        gain_ref, o_ref, nxg_ref, nssq_ref, e_s, inv_s = rest
    else:
        o_ref, e_s, inv_s = rest
    j = pl.program_id(1)
    n_tiles, _, bn = e_s.shape

    @pl.when(j == 0)
    def _():
        pb = p_ref[...].astype(BF16)
        ssq = jnp.zeros(inv_s.shape, F32)
        for t in range(n_tiles):
            e = jnp.dot(pb, wp_ref[:, t * bn:(t + 1) * bn].astype(BF16),
                        preferred_element_type=F32)
            e_s[t] = e
            ssq = ssq + jnp.sum(e * e, axis=-1, keepdims=True)
        inv_s[...] = lax.rsqrt(ssq / (n_tiles * bn) + EPS)

    xg = xg_ref[...]
    gate = jax.nn.sigmoid(jnp.dot(xg, wg_ref[...], preferred_element_type=F32)
                          * _inv_rms(ssq_ref, xg.shape[1]))
    e = (e_s[j] * inv_s[...]) * pg_ref[...]
    o = x_ref[...] + gate * e
    o_ref[...] = o
    if emit_next:
        _emit_normed(o, gain_ref, nxg_ref, nssq_ref, j == 0)


def _ple(xg, ssq, w_gate, p, w_proj_stack, layer, post_gain, x, next_gain):
    m, d = x.shape
    pd = p.shape[1]
    bm = _tile(m, 512)
    bn = _tile(d, 1024)
    emit_next = next_gain is not None
    tile = pl.BlockSpec((bm, bn), lambda i, j: (i, j))
    row = pl.BlockSpec((1, bn), lambda i, j: (0, j))
    stats = pl.BlockSpec((bm, LANES), lambda i, j: (i, 0))
    in_specs = [pl.BlockSpec((bm, d), lambda i, j: (i, 0)), stats,
                pl.BlockSpec((d, bn), lambda i, j: (0, j)),
                pl.BlockSpec((bm, pd), lambda i, j: (i, 0)),
                pl.BlockSpec((None, pd, d), lambda i, j: (layer, 0, 0)),
                row, tile]
    args = [xg, ssq, w_gate, p, w_proj_stack, post_gain.reshape(1, d), x]
    out_specs = [tile]
    out_shape = [jax.ShapeDtypeStruct((m, d), F32)]
    if emit_next:
        in_specs.append(row)
        args.append(next_gain.reshape(1, d))
        out_specs += [tile, stats]
        out_shape += [jax.ShapeDtypeStruct((m, d), BF16), jax.ShapeDtypeStruct((m, LANES), F32)]
    return pl.pallas_call(
        functools.partial(_ple_body, emit_next=emit_next),
        grid=(m // bm, d // bn),
        in_specs=in_specs,
        out_specs=out_specs,
        out_shape=out_shape,
        scratch_shapes=[pltpu.VMEM((d // bn, bm, bn), F32),
                        pltpu.VMEM((bm, 1), F32)],
        compiler_params=_params("parallel", "arbitrary"),
        name="ple",
    )(*args)


def kernel(x, p, ffn1_norm, ffn1_w_gate, ffn1_w_up, ffn1_w_down, mix_norm, w_in, q_norm, k_norm, lambda_q1, lambda_k1, lambda_q2, lambda_k2, attn_subln, conv_w, conv_norm, w_out, ffn2_norm, ffn2_w_gate, ffn2_w_up, ffn2_w_down, ple_w_proj, ple_post_norm, ple_gate_norm, ple_w_gate):
    bsz, seq, d = x.shape
    depth = p.shape[0]
    m = bsz * seq
    attn_width = w_out.shape[1] - conv_w.shape[2]
    conv_width = conv_w.shape[2]
    n_heads = attn_width // V_DIM
    assert w_in.shape[2] == 3 * attn_width + 3 * conv_width
    assert conv_width == attn_width, "a conv channel group rides along with each attention head"
    blk = _tile(seq, ATTN_BLOCK)
    sl, q_aug = _alibi_columns(n_heads)

    xf = x.reshape(m, d)
    xg, ssq = _prescale(xf, ffn1_norm[0])
    for i in range(depth):
        lam_init = 0.8 - 0.6 * math.exp(-0.3 * i)
        lam = (jnp.exp(jnp.sum(lambda_q1[i] * lambda_k1[i]))
               - jnp.exp(jnp.sum(lambda_q2[i] * lambda_k2[i])) + lam_init)
        scal = jnp.stack([lam, jnp.asarray(1.0 - lam_init, F32)]).astype(F32)

        xf, xg, ssq = _ffn(xf, xg, ssq, ffn1_w_gate, ffn1_w_up, ffn1_w_down, i, mix_norm[i])

        z = _in_proj(xg, ssq, w_in, i).reshape(bsz, seq, -1)
        qt, kp, vt, conv = _mixer_prep(z, q_norm[i], k_norm[i], q_aug, conv_w[i], conv_norm[i],
                                       n_heads, blk)
        attn = _diff_attention(qt, kp, vt, sl, scal, attn_subln[i])
        xf, xg, ssq = _out_proj(attn.reshape(m, attn_width), conv.reshape(m, conv_width),
                                w_out, i, xf, ffn2_norm[i])

        xf, xg, ssq = _ffn(xf, xg, ssq, ffn2_w_gate, ffn2_w_up, ffn2_w_down, i,
                           ple_gate_norm[i])

        next_gain = ffn1_norm[i + 1] if i + 1 < depth else None
        res = _ple(xg, ssq, _cast_layer(ple_w_gate, i), p[i].reshape(m, -1), ple_w_proj, i,
                   ple_post_norm[i], xf, next_gain)
        xf, xg, ssq = res if next_gain is not None else (res[0], None, None)
    return xf.reshape(bsz, seq, d)
```

```python
import functools
import math

import jax
import jax.numpy as jnp
from jax import lax
from jax.experimental import pallas as pl
from jax.experimental.pallas import tpu as pltpu

F32 = jnp.float32
BF16 = jnp.bfloat16

EPS = 1e-6
HALF_STEP = 0.5
QK_DIM = 64
V_DIM = 2 * QK_DIM
CONV_GROUP_DIM = 128
CONV_K = 3
LANES = 128
BF16_SUBLANES = 16
NEG_BIG = -1e30
LOG2E = 1.4426950408889634
SLOPE_PIECES = 3
POS_SPLIT = 256
ALIBI_COLS = 2 * SLOPE_PIECES
ONES_ROWS = 16
VT_ROWS = V_DIM + ONES_ROWS
ATTN_BLOCK = 512
ATTN_HEADS_PER_STEP = 4
PREP_BLOCKS = 2

MIB = 1024 * 1024
VMEM_LIMIT_MIB = 56


def _tile(dim, pref):
    t = min(dim, pref)
    while dim % t:
        t //= 2
    return t


def _params(*semantics):
    return pltpu.CompilerParams(dimension_semantics=semantics,
                                vmem_limit_bytes=VMEM_LIMIT_MIB * MIB)


def _emit_normed(o, gain_ref, xg_ref, ssq_ref, first):
    xg_ref[...] = (o * gain_ref[...]).astype(xg_ref.dtype)
    part = jnp.broadcast_to(jnp.sum(o * o, axis=-1, keepdims=True), ssq_ref.shape)

    @pl.when(first)
    def _():
        ssq_ref[...] = part

    @pl.when(jnp.logical_not(first))
    def _():
        ssq_ref[...] += part


def _inv_rms(ssq_ref, width):
    return lax.rsqrt(ssq_ref[:, 0:1] / width + EPS)


def _prescale_body(x_ref, g_ref, xg_ref, ssq_ref):
    x = x_ref[...]
    xg_ref[...] = (x * g_ref[...]).astype(xg_ref.dtype)
    ssq_ref[...] = jnp.broadcast_to(jnp.sum(x * x, axis=-1, keepdims=True), ssq_ref.shape)


def _prescale(x, gain):
    m, d = x.shape
    bm = _tile(m, 512)
    return pl.pallas_call(
        _prescale_body,
        grid=(m // bm,),
        in_specs=[pl.BlockSpec((bm, d), lambda i: (i, 0)),
                  pl.BlockSpec((1, d), lambda i: (0, 0))],
        out_specs=[pl.BlockSpec((bm, d), lambda i: (i, 0)),
                   pl.BlockSpec((bm, LANES), lambda i: (i, 0))],
        out_shape=[jax.ShapeDtypeStruct((m, d), BF16),
                   jax.ShapeDtypeStruct((m, LANES), F32)],
        compiler_params=_params("parallel"),
        name="prescale",
    )(x, gain.reshape(1, d))


def _cast_body(w_ref, o_ref):
    o_ref[...] = w_ref[...].astype(o_ref.dtype)


def _cast_layer(w_stack, layer):
    _, r, c = w_stack.shape
    br = _tile(r, 256)
    return pl.pallas_call(
        _cast_body,
        grid=(r // br,),
        in_specs=[pl.BlockSpec((None, br, c), lambda i: (layer, i, 0))],
        out_specs=pl.BlockSpec((br, c), lambda i: (i, 0)),
        out_shape=jax.ShapeDtypeStruct((r, c), BF16),
        compiler_params=_params("parallel"),
        name="cast_bf16",
    )(w_stack)


def _riding_cast(w_stack, layer, grid):
    _, r, c = w_stack.shape
    steps = grid[0] * grid[1]
    rows = r // steps
    if r % steps or rows % BF16_SUBLANES:
        return None
    in_spec = pl.BlockSpec((None, rows, c), lambda i, j: (layer, i * grid[1] + j, 0))
    out_spec = pl.BlockSpec((rows, c), lambda i, j: (i * grid[1] + j, 0))
    return in_spec, out_spec, jax.ShapeDtypeStruct((r, c), BF16)


def _gate_up_body(xg_ref, ssq_ref, wg_ref, wu_ref, *rest):
    if len(rest) == 3:
        wd_ref, a_ref, wdb_ref = rest
        wdb_ref[...] = wd_ref[...].astype(wdb_ref.dtype)
    else:
        (a_ref,) = rest
    xg = xg_ref[...]
    inv = _inv_rms(ssq_ref, xg.shape[1])
    g = jnp.dot(xg, wg_ref[...].astype(BF16), preferred_element_type=F32) * inv
    u = jnp.dot(xg, wu_ref[...].astype(BF16), preferred_element_type=F32) * inv
    a_ref[...] = ((g * jax.nn.sigmoid(g)) * u).astype(a_ref.dtype)


def _gate_up(xg, ssq, wg_stack, wu_stack, wd_stack, layer):
    m, d = xg.shape
    f = wg_stack.shape[2]
    bm = _tile(m, 1024)
    bf = _tile(f, 256)
    grid = (m // bm, f // bf)
    wspec = pl.BlockSpec((None, d, bf), lambda i, k: (layer, 0, k))
    in_specs = [pl.BlockSpec((bm, d), lambda i, k: (i, 0)),
                pl.BlockSpec((bm, LANES), lambda i, k: (i, 0)), wspec, wspec]
    out_specs = [pl.BlockSpec((bm, bf), lambda i, k: (i, k))]
    out_shape = [jax.ShapeDtypeStruct((m, f), BF16)]
    args = [xg, ssq, wg_stack, wu_stack]
    ride = _riding_cast(wd_stack, layer, grid)
    if ride is not None:
        in_specs.append(ride[0])
        out_specs.append(ride[1])
        out_shape.append(ride[2])
        args.append(wd_stack)
    res = pl.pallas_call(
        _gate_up_body,
        grid=grid,
        in_specs=in_specs,
        out_specs=out_specs,
        out_shape=out_shape,
        compiler_params=_params("arbitrary", "arbitrary"),
        name="ffn_gate_up",
    )(*args)
    return (res[0], res[1]) if ride is not None else (res[0], _cast_layer(wd_stack, layer))


def _down_body(a_ref, wd_ref, x_ref, gain_ref, *rest):
    if len(rest) == 5:
        w_ref, o_ref, xg_ref, ssq_ref, wb_ref = rest
        wb_ref[...] = w_ref[...].astype(wb_ref.dtype)
    else:
        o_ref, xg_ref, ssq_ref = rest
    o = x_ref[...] + HALF_STEP * jnp.dot(a_ref[...], wd_ref[...], preferred_element_type=F32)
    o_ref[...] = o
    _emit_normed(o, gain_ref, xg_ref, ssq_ref, pl.program_id(1) == 0)


def _down(a, wd, x, next_gain, cast_stack=None, layer=0):
    m, d = x.shape
    f = a.shape[1]
    bm = _tile(m, 512)
    bn = _tile(d, 512)
    grid = (m // bm, d // bn)
    tile = pl.BlockSpec((bm, bn), lambda i, j: (i, j))
    in_specs = [pl.BlockSpec((bm, f), lambda i, j: (i, 0)),
                pl.BlockSpec((f, bn), lambda i, j: (0, j)),
                tile,
                pl.BlockSpec((1, bn), lambda i, j: (0, j))]
    out_specs = [tile, tile, pl.BlockSpec((bm, LANES), lambda i, j: (i, 0))]
    out_shape = [jax.ShapeDtypeStruct((m, d), F32), jax.ShapeDtypeStruct((m, d), BF16),
                 jax.ShapeDtypeStruct((m, LANES), F32)]
    args = [a, wd, x, next_gain.reshape(1, d)]
    ride = _riding_cast(cast_stack, layer, grid) if cast_stack is not None else None
    if ride is not None:
        in_specs.append(ride[0])
        out_specs.append(ride[1])
        out_shape.append(ride[2])
        args.append(cast_stack)
    res = pl.pallas_call(
        _down_body,
        grid=grid,
        in_specs=in_specs,
        out_specs=out_specs,
        out_shape=out_shape,
        compiler_params=_params("arbitrary", "arbitrary"),
        name="ffn_down",
    )(*args)
    if cast_stack is None:
        return res[0], res[1], res[2], None
    return res[0], res[1], res[2], (res[3] if ride is not None else _cast_layer(cast_stack, layer))


def _ffn(x, xg, ssq, wg_stack, wu_stack, wd_stack, layer, next_gain, cast_stack=None):
    a, wd = _gate_up(xg, ssq, wg_stack, wu_stack, wd_stack, layer)
    return _down(a, wd, x, next_gain, cast_stack, layer)


def _in_proj_body(xg_ref, ssq_ref, w_ref, o_ref):
    xg = xg_ref[...]
    o_ref[...] = (jnp.dot(xg, w_ref[...].astype(BF16), preferred_element_type=F32)
                  * _inv_rms(ssq_ref, xg.shape[1]))


def _in_proj(xg, ssq, w_stack, layer):
    m, kd = xg.shape
    n = w_stack.shape[2]
    bm = _tile(m, 1024)
    bn = _tile(n, 512)
    return pl.pallas_call(
        _in_proj_body,
        grid=(m // bm, n // bn),
        in_specs=[pl.BlockSpec((bm, kd), lambda i, j: (i, 0)),
                  pl.BlockSpec((bm, LANES), lambda i, j: (i, 0)),
                  pl.BlockSpec((None, kd, bn), lambda i, j: (layer, 0, j))],
        out_specs=pl.BlockSpec((bm, bn), lambda i, j: (i, j)),
        out_shape=jax.ShapeDtypeStruct((m, n), F32),
        compiler_params=_params("parallel", "arbitrary"),
        name="in_proj",
    )(xg, ssq, w_stack)


def _mixer_prep_body(q_ref, k_ref, v_ref, qg_ref, kg_ref, qa_ref, b_ref, c_ref, u_ref, cw_ref,
                     cg_ref, qt_ref, kp_ref, vt_ref, y_ref, tail_s, *, blk):
    _gated_conv_tile(b_ref[...], c_ref[...], u_ref[...], cw_ref, cg_ref, y_ref, tail_s,
                     pl.program_id(2) == 0)
    rows = q_ref.shape[0]
    lane = lax.broadcasted_iota(jnp.int32, (1, V_DIM), 1)
    lo = lane < QK_DIM

    def group_norm(t, gain):
        t2 = t * t
        s_lo = jnp.sum(jnp.where(lo, t2, 0.0), axis=-1, keepdims=True)
        s_hi = jnp.sum(jnp.where(lo, 0.0, t2), axis=-1, keepdims=True)
        ms = jnp.where(lo, s_lo, s_hi) * (1.0 / QK_DIM)
        return (t * lax.rsqrt(ms + EPS)) * gain

    qn = group_norm(q_ref[...], qg_ref[...]) * (QK_DIM ** -0.5 * LOG2E)
    qt_ref[0] = jnp.where(lo, qn, qa_ref[0:1, :]).T.astype(BF16)
    qt_ref[1] = jnp.where(lo, qa_ref[1:2, :], qn).T.astype(BF16)

    kn = group_norm(k_ref[...], kg_ref[...])
    pos = lax.broadcasted_iota(jnp.int32, (rows, V_DIM), 0) & (blk - 1)
    pos_lo = (pos & (POS_SPLIT - 1)).astype(F32)
    pos_hi = (pos - (pos & (POS_SPLIT - 1))).astype(F32)
    col = jnp.where(lo, lane + QK_DIM, lane)
    ka = jnp.where((col >= QK_DIM) & (col < QK_DIM + SLOPE_PIECES), pos_lo,
                   jnp.where((col >= QK_DIM + SLOPE_PIECES) & (col < QK_DIM + ALIBI_COLS),
                             pos_hi, 0.0))
    kp_ref[0] = jnp.where(lo, kn, ka).astype(BF16)
    kp_ref[1] = jnp.where(lo, ka, kn).astype(BF16)

    for t in range(rows // blk):
        vt_ref[t, 0:V_DIM, :] = v_ref[t * blk:(t + 1) * blk, :].T.astype(BF16)
        vt_ref[t, V_DIM:, :] = jnp.ones((ONES_ROWS, blk), BF16)


def _mixer_prep(z, q_gain, k_gain, q_aug, conv_w, conv_gain, n_heads, blk):
    bsz, seq, _ = z.shape
    slab = lambda k: pl.BlockSpec((None, rows, V_DIM), lambda b, h, s: (b, s, k * n_heads + h))
    assert blk & (blk - 1) == 0
    nb = seq // blk
    per = PREP_BLOCKS if nb % PREP_BLOCKS == 0 else 1
    rows = per * blk
    gain_spec = pl.BlockSpec((1, V_DIM), lambda b, h, s: (0, 0))
    return pl.pallas_call(
        functools.partial(_mixer_prep_body, blk=blk),
        grid=(bsz, n_heads, nb // per),
        in_specs=[slab(0), slab(1), slab(2), gain_spec, gain_spec,
                  pl.BlockSpec((None, 2, V_DIM), lambda b, h, s: (h, 0, 0)),
                  slab(3), slab(4), slab(5),
                  pl.BlockSpec((CONV_K, V_DIM), lambda b, h, s: (0, h)),
                  gain_spec],
        out_specs=[pl.BlockSpec((None, None, 2, V_DIM, rows), lambda b, h, s: (b, h, 0, 0, s)),
                   pl.BlockSpec((None, None, 2, rows, V_DIM), lambda b, h, s: (b, h, 0, s, 0)),
                   pl.BlockSpec((None, None, per, VT_ROWS, blk), lambda b, h, s: (b, h, s, 0, 0)),
                   pl.BlockSpec((None, rows, V_DIM), lambda b, h, s: (b, s, h))],
        out_shape=[jax.ShapeDtypeStruct((bsz, n_heads, 2, V_DIM, seq), BF16),
                   jax.ShapeDtypeStruct((bsz, n_heads, 2, seq, V_DIM), BF16),
                   jax.ShapeDtypeStruct((bsz, n_heads, nb, VT_ROWS, blk), BF16),
                   jax.ShapeDtypeStruct((bsz, seq, n_heads * V_DIM), BF16)],
        scratch_shapes=[pltpu.VMEM((8, V_DIM), F32)],
        compiler_params=_params("parallel", "parallel", "arbitrary"),
        name="mixer_prep",
    )(z, z, z, jnp.tile(q_gain, 2).reshape(1, V_DIM), jnp.tile(k_gain, 2).reshape(1, V_DIM), q_aug,
      z, z, z, conv_w, conv_gain.reshape(1, CONV_GROUP_DIM))


def _attn_body(scal_ref, sl_ref, qt_ref, k_ref, vt_ref, sg_ref, o_ref,
               m_s, acc_s, st_a, st_b, *, blk, hps):
    hg = pl.program_id(1)
    qi = pl.program_id(2)
    n_chains = 2 * hps
    m_s[...] = jnp.full_like(m_s, NEG_BIG)
    acc_s[...] = jnp.zeros_like(acc_s)
    st_bufs = (st_a, st_b)

    def scores(j, dst):
        r = pl.multiple_of(j * blk, blk)
        for c in range(n_chains):
            st_bufs[dst][c] = jnp.dot(k_ref[c // 2, c % 2, pl.ds(r, blk), :], qt_ref[c // 2, c % 2],
                                      preferred_element_type=F32)

    def absorb(j, src, diagonal):
        for c in range(n_chains):
            off = sl_ref[hg * hps + c // 2] * ((j - qi) * blk).astype(F32)
            st = st_bufs[src][c]
            if diagonal:
                key = lax.broadcasted_iota(jnp.int32, st.shape, 0)
                qry = lax.broadcasted_iota(jnp.int32, st.shape, 1)
                st = jnp.where(key <= qry, st, -jnp.inf)
            m_old = m_s[c]
            m_new = jnp.maximum(m_old, jnp.max(st, axis=0, keepdims=True) + off)
            alpha = jnp.exp2(m_old - m_new)
            pt = jnp.exp2(st - (m_new - off)).astype(BF16)
            acc_s[c] = alpha * acc_s[c] + jnp.dot(vt_ref[c // 2, j], pt,
                                                  preferred_element_type=F32)
            m_s[c] = m_new

    scores(0, 0)

    def pair(t, carry):
        j = 2 * t
        scores(j + 1, 1)
        absorb(j, 0, False)
        scores(j + 2, 0)
        absorb(j + 1, 1, False)
        return carry
    lax.fori_loop(0, qi // 2, pair, 0)

    @pl.when(qi % 2 == 0)
    def _():
        absorb(qi, 0, True)

    @pl.when(qi % 2 == 1)
    def _():
        scores(qi, 1)
        absorb(qi - 1, 0, False)
        absorb(qi, 1, True)

    lam = scal_ref[0]
    for hh in range(hps):
        a0 = acc_s[2 * hh]
        a1 = acc_s[2 * hh + 1]
        o = a0[:V_DIM] / a0[V_DIM:V_DIM + 1] - lam * (a1[:V_DIM] / a1[V_DIM:V_DIM + 1])
        ms = jnp.mean(o * o, axis=0, keepdims=True)
        y = ((o * lax.rsqrt(ms + EPS)) * sg_ref[...]) * scal_ref[1]
        o_ref[:, hh * V_DIM:(hh + 1) * V_DIM] = y.T.astype(o_ref.dtype)


def _bf16_part(x):
    bits = lax.bitcast_convert_type(x, jnp.uint32) & jnp.uint32(0xFFFF0000)
    return lax.bitcast_convert_type(bits, F32)


def _alibi_columns(n_heads):
    slopes = jnp.exp2(-8.0 * jnp.arange(1, n_heads + 1, dtype=F32) / n_heads)
    sl = slopes * LOG2E
    a1 = _bf16_part(sl)
    a2 = _bf16_part(sl - a1)
    a3 = _bf16_part(sl - a1 - a2)
    pieces = jnp.stack([a1, a2, a3, a1, a2, a3], axis=1)
    q_aug = jnp.zeros((n_heads, 2, V_DIM), F32)
    q_aug = q_aug.at[:, 0, QK_DIM:QK_DIM + ALIBI_COLS].set(pieces)
    q_aug = q_aug.at[:, 1, 0:ALIBI_COLS].set(pieces)
    return a1 + a2 + a3, q_aug


def _diff_attention(qt, kp, vt, sl, scal, subln_gain):
    bsz, n_heads, _, _, seq = qt.shape
    blk = vt.shape[-1]
    hps = ATTN_HEADS_PER_STEP if n_heads % ATTN_HEADS_PER_STEP == 0 else 1
    n_chains = 2 * hps
    smem = pl.BlockSpec(memory_space=pltpu.SMEM)
    return pl.pallas_call(
        functools.partial(_attn_body, blk=blk, hps=hps),
        grid=(bsz, n_heads // hps, seq // blk),
        in_specs=[smem, smem,
                  pl.BlockSpec((None, hps, 2, V_DIM, blk), lambda b, h, i: (b, h, 0, 0, i)),
                  pl.BlockSpec((None, hps, 2, seq, V_DIM), lambda b, h, i: (b, h, 0, 0, 0),
                               pipeline_mode=pl.Buffered(1)),
                  pl.BlockSpec((None, hps, seq // blk, VT_ROWS, blk), lambda b, h, i: (b, h, 0, 0, 0),
                               pipeline_mode=pl.Buffered(1)),
                  pl.BlockSpec((V_DIM, 1), lambda b, h, i: (0, 0))],
        out_specs=pl.BlockSpec((None, blk, hps * V_DIM), lambda b, h, i: (b, i, h)),
        out_shape=jax.ShapeDtypeStruct((bsz, seq, n_heads * V_DIM), BF16),
        scratch_shapes=[pltpu.VMEM((n_chains, 1, blk), F32),
                        pltpu.VMEM((n_chains, VT_ROWS, blk), F32),
                        pltpu.VMEM((n_chains, blk, blk), F32),
                        pltpu.VMEM((n_chains, blk, blk), F32)],
        compiler_params=_params("parallel", "parallel", "arbitrary"),
        name="diff_attn",
    )(scal, sl, qt, kp, vt, subln_gain.reshape(V_DIM, 1))


def _gated_conv_tile(b, c, u, w_ref, g_ref, o_ref, tail_s, first):
    rows, cols = o_ref.shape

    @pl.when(first)
    def _():
        tail_s[...] = jnp.zeros_like(tail_s)

    cu = c * u
    row = lax.broadcasted_iota(jnp.int32, (rows, cols), 0)
    tail = tail_s[...]
    prev1 = jnp.where(row == 0, tail[7:8, :], pltpu.roll(cu, 1, axis=0))
    prev2 = jnp.where(row == 0, tail[6:7, :],
                      jnp.where(row == 1, tail[7:8, :], pltpu.roll(cu, 2, axis=0)))
    acc = prev2 * w_ref[0:1, :]
    acc = acc + prev1 * w_ref[1:2, :]
    acc = acc + cu * w_ref[2:3, :]
    y = b * acc
    tail_s[...] = cu[rows - 8:, :]
    for gi in range(cols // CONV_GROUP_DIM):
        sl = slice(gi * CONV_GROUP_DIM, (gi + 1) * CONV_GROUP_DIM)
        yg = y[:, sl]
        ms = jnp.mean(yg * yg, axis=-1, keepdims=True)
        o_ref[:, sl] = ((yg * lax.rsqrt(ms + EPS)) * g_ref[...]).astype(o_ref.dtype)


def _out_proj_body(a_ref, y_ref, wa_ref, wc_ref, x_ref, gain_ref, o_ref, xg_ref, ssq_ref):
    mix = (jnp.dot(a_ref[...], wa_ref[...].astype(BF16), preferred_element_type=F32)
           + jnp.dot(y_ref[...], wc_ref[...].astype(BF16), preferred_element_type=F32))
    o = x_ref[...] + mix
    o_ref[...] = o
    _emit_normed(o, gain_ref, xg_ref, ssq_ref, pl.program_id(1) == 0)


def _out_proj(attn, conv, w_stack, layer, x, next_gain):
    m, d = x.shape
    bm = _tile(m, 1024)
    bn = _tile(d, 512)
    ka, kc = attn.shape[1], conv.shape[1]
    assert ka == kc, "row blocks of the output projection are indexed in units of one head group"
    tile = pl.BlockSpec((bm, bn), lambda i, j: (i, j))
    return pl.pallas_call(
        _out_proj_body,
        grid=(m // bm, d // bn),
        in_specs=[pl.BlockSpec((bm, ka), lambda i, j: (i, 0)),
                  pl.BlockSpec((bm, kc), lambda i, j: (i, 0)),
                  pl.BlockSpec((None, ka, bn), lambda i, j: (layer, 0, j)),
                  pl.BlockSpec((None, kc, bn), lambda i, j: (layer, 1, j)),
                  tile,
                  pl.BlockSpec((1, bn), lambda i, j: (0, j))],
        out_specs=[tile, tile, pl.BlockSpec((bm, LANES), lambda i, j: (i, 0))],
        out_shape=[jax.ShapeDtypeStruct((m, d), F32), jax.ShapeDtypeStruct((m, d), BF16),
                   jax.ShapeDtypeStruct((m, LANES), F32)],
        compiler_params=_params("parallel", "arbitrary"),
        name="out_proj",
    )(attn, conv, w_stack, w_stack, x, next_gain.reshape(1, d))


def _ple_body(xg_ref, ssq_ref, wg_ref, p_ref, wp_ref, pg_ref, x_ref, *rest, emit_next):
    if emit_next:
        gain_ref, o_ref, nxg_ref, nssq_ref, e_s, inv_s = rest
    else:
        o_ref, e_s, inv_s = rest
    j = pl.program_id(1)
    n_tiles, _, bn = e_s.shape

    @pl.when(j == 0)
    def _():
        pb = p_ref[...].astype(BF16)
        ssq = jnp.zeros(inv_s.shape, F32)
        for t in range(n_tiles):
            e = jnp.dot(pb, wp_ref[:, t * bn:(t + 1) * bn].astype(BF16),
                        preferred_element_type=F32)
            e_s[t] = e
            ssq = ssq + jnp.sum(e * e, axis=-1, keepdims=True)
        inv_s[...] = lax.rsqrt(ssq / (n_tiles * bn) + EPS)

    xg = xg_ref[...]
    gate = jax.nn.sigmoid(jnp.dot(xg, wg_ref[...], preferred_element_type=F32)
                          * _inv_rms(ssq_ref, xg.shape[1]))
    e = (e_s[j] * inv_s[...]) * pg_ref[...]
    o = x_ref[...] + gate * e
    o_ref[...] = o
    if emit_next:
        _emit_normed(o, gain_ref, nxg_ref, nssq_ref, j == 0)


def _ple(xg, ssq, w_gate, p, w_proj_stack, layer, post_gain, x, next_gain):
    m, d = x.shape
    pd = p.shape[1]
    bm = _tile(m, 512)
    bn = _tile(d, 1024)
    emit_next = next_gain is not None
    tile = pl.BlockSpec((bm, bn), lambda i, j: (i, j))
    row = pl.BlockSpec((1, bn), lambda i, j: (0, j))
    stats = pl.BlockSpec((bm, LANES), lambda i, j: (i, 0))
    in_specs = [pl.BlockSpec((bm, d), lambda i, j: (i, 0)), stats,
                pl.BlockSpec((d, bn), lambda i, j: (0, j)),
                pl.BlockSpec((bm, pd), lambda i, j: (i, 0)),
                pl.BlockSpec((None, pd, d), lambda i, j: (layer, 0, 0)),
                row, tile]
    args = [xg, ssq, w_gate, p, w_proj_stack, post_gain.reshape(1, d), x]
    out_specs = [tile]
    out_shape = [jax.ShapeDtypeStruct((m, d), F32)]
    if emit_next:
        in_specs.append(row)
        args.append(next_gain.reshape(1, d))
        out_specs += [tile, stats]
        out_shape += [jax.ShapeDtypeStruct((m, d), BF16), jax.ShapeDtypeStruct((m, LANES), F32)]
    return pl.pallas_call(
        functools.partial(_ple_body, emit_next=emit_next),
        grid=(m // bm, d // bn),
        in_specs=in_specs,
        out_specs=out_specs,
        out_shape=out_shape,
        scratch_shapes=[pltpu.VMEM((d // bn, bm, bn), F32),
                        pltpu.VMEM((bm, 1), F32)],
        compiler_params=_params("parallel", "arbitrary"),
        name="ple",
    )(*args)


def kernel(x, p, ffn1_norm, ffn1_w_gate, ffn1_w_up, ffn1_w_down, mix_norm, w_in, q_norm, k_norm, lambda_q1, lambda_k1, lambda_q2, lambda_k2, attn_subln, conv_w, conv_norm, w_out, ffn2_norm, ffn2_w_gate, ffn2_w_up, ffn2_w_down, ple_w_proj, ple_post_norm, ple_gate_norm, ple_w_gate):
    bsz, seq, d = x.shape
    depth = p.shape[0]
    m = bsz * seq
    attn_width = w_out.shape[1] - conv_w.shape[2]
    conv_width = conv_w.shape[2]
    n_heads = attn_width // V_DIM
    assert w_in.shape[2] == 3 * attn_width + 3 * conv_width
    assert conv_width == attn_width, "a conv channel group rides along with each attention head"
    blk = _tile(seq, ATTN_BLOCK)
    sl, q_aug = _alibi_columns(n_heads)

    xf = x.reshape(m, d)
    xg, ssq = _prescale(xf, ffn1_norm[0])
    for i in range(depth):
        lam_init = 0.8 - 0.6 * math.exp(-0.3 * i)
        lam = (jnp.exp(jnp.sum(lambda_q1[i] * lambda_k1[i]))
               - jnp.exp(jnp.sum(lambda_q2[i] * lambda_k2[i])) + lam_init)
        scal = jnp.stack([lam, jnp.asarray(1.0 - lam_init, F32)]).astype(F32)

        xf, xg, ssq, _ = _ffn(xf, xg, ssq, ffn1_w_gate, ffn1_w_up, ffn1_w_down, i, mix_norm[i])

        z = _in_proj(xg, ssq, w_in, i).reshape(bsz, seq, -1)
        qt, kp, vt, conv = _mixer_prep(z, q_norm[i], k_norm[i], q_aug, conv_w[i], conv_norm[i],
                                       n_heads, blk)
        attn = _diff_attention(qt, kp, vt, sl, scal, attn_subln[i])
        xf, xg, ssq = _out_proj(attn.reshape(m, attn_width), conv.reshape(m, conv_width),
                                w_out, i, xf, ffn2_norm[i])

        xf, xg, ssq, ple_wg = _ffn(xf, xg, ssq, ffn2_w_gate, ffn2_w_up, ffn2_w_down, i,
                                   ple_gate_norm[i], ple_w_gate)

        next_gain = ffn1_norm[i + 1] if i + 1 < depth else None
        res = _ple(xg, ssq, ple_wg, p[i].reshape(m, -1), ple_w_proj, i,
                   ple_post_norm[i], xf, next_gain)
        xf, xg, ssq = res if next_gain is not None else (res[0], None, None)
    return xf.reshape(bsz, seq, d)
```
